```python
import jax, jax.numpy as jnp
from jax import lax
import numpy as np

D_MODEL = 1024
BATCH = 8
SEQ = 4096
DEPTH = 2

HEAD_DIM = 64
N_HEADS_TOTAL = D_MODEL // HEAD_DIM
N_Q_HEADS = N_HEADS_TOTAL // 2
N_KV_HEADS = max(1, N_Q_HEADS // 4)
Q_PER_KV = N_Q_HEADS // N_KV_HEADS
ATTN_WIDTH = N_Q_HEADS * HEAD_DIM
KV_WIDTH = N_KV_HEADS * HEAD_DIM
N_SGU_GROUPS = N_HEADS_TOTAL - N_Q_HEADS
SGU_GROUP_DIM = HEAD_DIM
SGU_WIDTH = N_SGU_GROUPS * SGU_GROUP_DIM
MIX_WIDTH = ATTN_WIDTH + SGU_WIDTH
IN_WIDTH = ATTN_WIDTH + 2 * KV_WIDTH + 2 * SGU_WIDTH
WINDOW = 128
BLOCK = 128
SGU_CHUNK = 128
ROPE_THETA = 500000.0
ROPE_DIM = HEAD_DIM // 4
D_FF_DENSE = ((8 * D_MODEL // 3 + 255) // 256) * 256
N_EXPERTS = 8
TOP_K = 2
D_FF_EXPERT = 7 * D_MODEL // 2
EPS = 1e-6
N_DENSE = (DEPTH + 1) // 2
N_MOE = DEPTH // 2

kernel_name = "hybrid_swa_sink_sgu_moe_block"


def rmsnorm(x, g):
    xf = x.astype(jnp.float32)
    y = xf * lax.rsqrt(jnp.mean(xf * xf, axis=-1, keepdims=True) + EPS)
    return (y * g.astype(jnp.float32)).astype(x.dtype)


def partial_rope(t, positions):
    half = ROPE_DIM // 2
    inv_freq = jnp.power(jnp.float32(ROPE_THETA), -jnp.arange(0, ROPE_DIM, 2, dtype=jnp.float32) / ROPE_DIM)
    ang = positions.astype(jnp.float32)[..., None] * inv_freq
    cos = jnp.cos(ang)[:, :, None, :]
    sin = jnp.sin(ang)[:, :, None, :]
    tr = t[..., :ROPE_DIM].astype(jnp.float32)
    t1, t2 = tr[..., :half], tr[..., half:]
    rot = jnp.concatenate([t1 * cos - t2 * sin, t2 * cos + t1 * sin], axis=-1)
    return jnp.concatenate([rot.astype(t.dtype), t[..., ROPE_DIM:]], axis=-1)


def sliding_window_attention(q, k, v, sinks):
    B, S = q.shape[0], q.shape[1]
    nb = S // BLOCK
    qb = q.reshape(B, nb, BLOCK, N_KV_HEADS, Q_PER_KV, HEAD_DIM)

    def band(t):
        tb = t.reshape(B, nb, BLOCK, N_KV_HEADS, HEAD_DIM)
        prev = jnp.pad(tb, ((0, 0), (1, 0), (0, 0), (0, 0), (0, 0)))[:, :-1]
        return jnp.concatenate([prev, tb], axis=2)

    kb, vb = band(k), band(v)
    scores = jnp.einsum('bnqhgd,bnkhd->bnhgqk', qb, kb).astype(jnp.float32) * (HEAD_DIM ** -0.5)
    qi = jnp.arange(BLOCK)[:, None]
    kj = jnp.arange(2 * BLOCK)[None, :]
    dist = qi + BLOCK - kj
    in_window = (dist >= 0) & (dist < WINDOW)
    exists = (jnp.arange(nb)[:, None, None] > 0) | (kj >= BLOCK)[None]
    mask = in_window[None] & exists
    scores = jnp.where(mask[None, :, None, None], scores, -1e30)
    sink = sinks.astype(jnp.float32).reshape(N_KV_HEADS, Q_PER_KV)[None, None, :, :, None, None]
    sink = jnp.broadcast_to(sink, scores.shape[:-1] + (1,))
    probs = jax.nn.softmax(jnp.concatenate([scores, sink], axis=-1), axis=-1)[..., :-1]
    out = jnp.einsum('bnhgqk,bnkhd->bnqhgd', probs.astype(v.dtype), vb)
    return out.reshape(B, S, N_Q_HEADS * HEAD_DIM)


def spatial_gating(u, v, ln_g, ln_b, w_s, b_s):
    B, S = v.shape[0], v.shape[1]
    nc = S // SGU_CHUNK
    vg = v.reshape(B, S, N_SGU_GROUPS, SGU_GROUP_DIM).astype(jnp.float32)
    mu = jnp.mean(vg, axis=-1, keepdims=True)
    var = jnp.mean(jnp.square(vg - mu), axis=-1, keepdims=True)
    vn = (vg - mu) * lax.rsqrt(var + EPS)
    vn = vn * ln_g.astype(jnp.float32).reshape(N_SGU_GROUPS, SGU_GROUP_DIM) + ln_b.astype(jnp.float32).reshape(N_SGU_GROUPS, SGU_GROUP_DIM)
    vn = vn.astype(v.dtype).reshape(B, nc, SGU_CHUNK, N_SGU_GROUPS, SGU_GROUP_DIM)
    causal = jnp.tril(jnp.ones((SGU_CHUNK, SGU_CHUNK), dtype=bool))
    w = jnp.where(causal[None], w_s, jnp.zeros_like(w_s))
    mixed = jnp.einsum('gts,bcsgd->bctgd', w, vn) + b_s.T[None, None, :, :, None]
    ug = u.reshape(B, nc, SGU_CHUNK, N_SGU_GROUPS, SGU_GROUP_DIM)
    return (ug * mixed).reshape(B, S, SGU_WIDTH)


def swiglu(x, w_gate, w_up, w_down):
    return (jax.nn.silu(x @ w_gate) * (x @ w_up)) @ w_down


def moe_swiglu(x, w_router, e_gate, e_up, e_down):
    B, S, D = x.shape
    xt = x.reshape(B * S, D)
    logits = (xt @ w_router).astype(jnp.float32)
    top_vals, top_idx = lax.top_k(logits, TOP_K)
    top_w = jax.nn.softmax(top_vals, axis=-1)
    gates = jnp.sum(jax.nn.one_hot(top_idx, N_EXPERTS, dtype=jnp.float32) * top_w[..., None], axis=1)
    out = jnp.zeros((B * S, D), dtype=jnp.float32)
    for e in range(N_EXPERTS):
        h = jax.nn.silu(xt @ e_gate[e]) * (xt @ e_up[e])
        out = out + gates[:, e:e + 1] * (h @ e_down[e]).astype(jnp.float32)
    return out.astype(x.dtype).reshape(B, S, D)


def setup_inputs(seed: int = 0) -> dict:
    key = jax.random.key(seed)
    ks = jax.random.split(key, 24)
    f32 = jnp.float32

    def w(k, shape, fan_in):
        return jax.random.normal(k, shape, f32) * (fan_in ** -0.5)

    def gain(k, shape):
        return 1.0 + 0.1 * jax.random.normal(k, shape, f32)

    x = jax.random.normal(ks[0], (BATCH, SEQ, D_MODEL), f32)
    offset = jax.random.randint(ks[1], (BATCH, 1), 0, 1024, dtype=jnp.int32)
    positions = offset + jnp.arange(SEQ, dtype=jnp.int32)[None, :]
    return {
        "x": x,
        "positions": positions,
        "pre_mix_norm": gain(ks[2], (DEPTH, D_MODEL)),
        "w_in": w(ks[3], (DEPTH, D_MODEL, IN_WIDTH), D_MODEL),
        "attn_sinks": jax.random.normal(ks[4], (DEPTH, N_Q_HEADS), f32),
        "sgu_ln_g": gain(ks[5], (DEPTH, SGU_WIDTH)),
        "sgu_ln_b": 0.1 * jax.random.normal(ks[6], (DEPTH, SGU_WIDTH), f32),
        "sgu_w": w(ks[7], (DEPTH, N_SGU_GROUPS, SGU_CHUNK, SGU_CHUNK), SGU_CHUNK),
        "sgu_b": gain(ks[8], (DEPTH, N_SGU_GROUPS, SGU_CHUNK)),
        "attn_out_norm": gain(ks[9], (DEPTH, ATTN_WIDTH)),
        "sgu_out_norm": gain(ks[10], (DEPTH, SGU_WIDTH)),
        "w_out": w(ks[11], (DEPTH, MIX_WIDTH, D_MODEL), MIX_WIDTH),
        "post_mix_norm": gain(ks[12], (DEPTH, D_MODEL)),
        "pre_ffn_norm": gain(ks[13], (DEPTH, D_MODEL)),
        "post_ffn_norm": gain(ks[14], (DEPTH, D_MODEL)),
        "ffn_w_gate": w(ks[15], (N_DENSE, D_MODEL, D_FF_DENSE), D_MODEL),
        "ffn_w_up": w(ks[16], (N_DENSE, D_MODEL, D_FF_DENSE), D_MODEL),
        "ffn_w_down": w(ks[17], (N_DENSE, D_FF_DENSE, D_MODEL), D_FF_DENSE),
        "moe_router": w(ks[18], (N_MOE, D_MODEL, N_EXPERTS), D_MODEL),
        "moe_w_gate": w(ks[19], (N_MOE, N_EXPERTS, D_MODEL, D_FF_EXPERT), D_MODEL),
        "moe_w_up": w(ks[20], (N_MOE, N_EXPERTS, D_MODEL, D_FF_EXPERT), D_MODEL),
        "moe_w_down": w(ks[21], (N_MOE, N_EXPERTS, D_FF_EXPERT, D_MODEL), D_FF_EXPERT),
    }


def reference(x, positions, pre_mix_norm, w_in, attn_sinks, sgu_ln_g, sgu_ln_b, sgu_w, sgu_b,
              attn_out_norm, sgu_out_norm, w_out, post_mix_norm, pre_ffn_norm, post_ffn_norm,
              ffn_w_gate, ffn_w_up, ffn_w_down, moe_router, moe_w_gate, moe_w_up, moe_w_down):
    B, S, _ = x.shape
    cuts = [ATTN_WIDTH, ATTN_WIDTH + KV_WIDTH, ATTN_WIDTH + 2 * KV_WIDTH, ATTN_WIDTH + 2 * KV_WIDTH + SGU_WIDTH]
    h = x
    for layer in range(DEPTH):
        hn = rmsnorm(h, pre_mix_norm[layer])
        z = hn @ w_in[layer]
        q, k, v, zu, zv = jnp.split(z, cuts, axis=-1)
        q = partial_rope(q.reshape(B, S, N_Q_HEADS, HEAD_DIM), positions)
        k = partial_rope(k.reshape(B, S, N_KV_HEADS, HEAD_DIM), positions)
        v = v.reshape(B, S, N_KV_HEADS, HEAD_DIM)
        attn = sliding_window_attention(q, k, v, attn_sinks[layer])
        sgu = spatial_gating(jax.nn.gelu(zu), jax.nn.gelu(zv), sgu_ln_g[layer], sgu_ln_b[layer],
                             sgu_w[layer], sgu_b[layer])
        merged = jnp.concatenate([rmsnorm(attn, attn_out_norm[layer]),
                                  rmsnorm(sgu, sgu_out_norm[layer])], axis=-1)
        h = h + rmsnorm(merged @ w_out[layer], post_mix_norm[layer])
        y = rmsnorm(h, pre_ffn_norm[layer])
        if layer % 2 == 0:
            i = layer // 2
            f = swiglu(y, ffn_w_gate[i], ffn_w_up[i], ffn_w_down[i])
        else:
            i = layer // 2
            f = moe_swiglu(y, moe_router[i], moe_w_gate[i], moe_w_up[i], moe_w_down[i])
        h = h + rmsnorm(f, post_ffn_norm[layer])
    return h
```

```python
import functools

import numpy as np
import jax
import jax.numpy as jnp
from jax import lax
from jax.experimental import pallas as pl
from jax.experimental.pallas import tpu as pltpu

F32 = jnp.float32
BF16 = jnp.bfloat16
I32 = jnp.int32
U32 = jnp.uint32

V7X_LANES = 128
V7X_SUBLANES = 8
V7X_VMEM_BUDGET_BYTES = 56 * 1024 * 1024

HEAD_DIM = 64
Q_PER_KV = 4
WINDOW = 128
BLOCK = 128
ROPE_DIM = HEAD_DIM // 4
ROPE_THETA = 500000.0
N_EXPERTS = 8
TOP_K = 2
EPS = 1e-6
NEG_BIG = -1e30

TOKEN_TILE = 512
BLOCKS_PER_TILE = TOKEN_TILE // BLOCK
EXPERT_ROW_TILE = 512
ROUTE_CHUNK = 512
DISPATCH_TILE = 1024
COMBINE_TILE = 256
TILE_INFO_LEN = 256


def _vmem_limit(block_bytes, scratch_bytes=0, temp_bytes=0):
    need = 2 * int(block_bytes) + int(scratch_bytes) + int(temp_bytes)
    return int(min(max(need, 16 * 1024 * 1024), V7X_VMEM_BUDGET_BYTES))


def _nbytes(shape, dtype):
    return int(np.prod(shape)) * jnp.dtype(dtype).itemsize


def _params(vmem_bytes, n_grid_dims):
    return pltpu.CompilerParams(
        dimension_semantics=("arbitrary",) * n_grid_dims,
        vmem_limit_bytes=vmem_bytes,
    )


def _rmsnorm(x, g):
    return x * lax.rsqrt(jnp.mean(x * x, axis=-1, keepdims=True) + EPS) * g


def _gelu_tanh(x):
    c = np.float32(np.sqrt(2.0 / np.pi))
    return x * (0.5 * (1.0 + jnp.tanh(c * (x + 0.044715 * (x * x * x)))))


def _silu(x):
    return x * (1.0 / (1.0 + jnp.exp(-x)))


def _split_bf16(x):
    hi = x.astype(BF16)
    lo = (x - hi.astype(F32)).astype(BF16)
    return hi, lo


def _dot(a, b):
    return jnp.dot(a, b, preferred_element_type=F32)


def _dot_nt(a, b):
    return lax.dot_general(a, b, (((1,), (1,)), ((), ())), preferred_element_type=F32)


def _in_proj_kernel(h_ref, pos_ref, g_ref, w_ref, invf_ref, gsum_ref, gbc_ref, lng_ref, lnb_ref,
                    q_ref, kv_ref, u_ref, vn_ref, *, attn_w, kv_w, sgu_w):
    hn = _rmsnorm(h_ref[...], g_ref[...]).astype(BF16)
    z = _dot(hn, w_ref[...])

    ang = pos_ref[...].astype(F32) * invf_ref[...]
    lane = lax.broadcasted_iota(I32, (1, V7X_LANES), 1)
    hl = lane % HEAD_DIM
    cos = jnp.cos(ang)
    sin = jnp.sin(ang)
    half = ROPE_DIM // 2
    sin_a = jnp.where(hl < half, -sin, 0.0)
    sin_b = jnp.where((hl >= half) & (hl < ROPE_DIM), sin, 0.0)

    def rope(t):
        return (t * cos + pltpu.roll(t, V7X_LANES - half, 1) * sin_a
                + pltpu.roll(t, half, 1) * sin_b)

    scale = np.float32(HEAD_DIM ** -0.5)
    for c in range(attn_w // V7X_LANES):
        sl = slice(c * V7X_LANES, (c + 1) * V7X_LANES)
        q_ref[:, sl] = (rope(z[:, sl]) * scale).astype(BF16)

    low = lane < HEAD_DIM
    k = rope(z[:, attn_w:attn_w + kv_w])
    v = z[:, attn_w + kv_w:attn_w + 2 * kv_w]
    kr = pltpu.roll(k, HEAD_DIM, 1)
    vr = pltpu.roll(v, HEAD_DIM, 1)
    kv_ref[:, 0 * V7X_LANES:1 * V7X_LANES] = jnp.where(low, k, kr).astype(BF16)
    kv_ref[:, 1 * V7X_LANES:2 * V7X_LANES] = jnp.where(low, kr, k).astype(BF16)
    kv_ref[:, 2 * V7X_LANES:3 * V7X_LANES] = jnp.where(low, v, vr).astype(BF16)
    kv_ref[:, 3 * V7X_LANES:4 * V7X_LANES] = jnp.where(low, vr, v).astype(BF16)

    o = attn_w + 2 * kv_w
    u_ref[...] = _gelu_tanh(z[:, o:o + sgu_w])

    zv = _gelu_tanh(z[:, o + sgu_w:o + 2 * sgu_w])
    gsum = gsum_ref[...]
    gbc = gbc_ref[...]

    def group_mean(a):
        hi, lo = _split_bf16(a)
        s = (_dot(hi, gsum) + _dot(lo, gsum)) * np.float32(1.0 / HEAD_DIM)
        hi, lo = _split_bf16(s)
        return _dot(hi, gbc) + _dot(lo, gbc)

    d = zv - group_mean(zv)
    var = group_mean(d * d)
    vn_ref[...] = (d * lax.rsqrt(var + EPS) * lng_ref[...] + lnb_ref[...]).astype(BF16)


def _in_proj(h, pos, g_pre, w_in, invf, gsum, gbc, ln_g, ln_b, *, attn_w, kv_w, sgu_w):
    T, D = h.shape
    tm = TOKEN_TILE
    in_w = w_in.shape[1]
    row = lambda i: (i, 0)
    const = lambda i: (0, 0)
    blocks = (_nbytes((tm, D), F32) + _nbytes((tm, V7X_LANES), I32) + _nbytes((D, in_w), BF16)
              + _nbytes((tm, attn_w), BF16) + _nbytes((tm, 4 * V7X_LANES), BF16)
              + _nbytes((tm, sgu_w), F32) + _nbytes((tm, sgu_w), BF16))
    temps = 6 * _nbytes((tm, in_w), F32)
    kern = functools.partial(_in_proj_kernel, attn_w=attn_w, kv_w=kv_w, sgu_w=sgu_w)
    return pl.pallas_call(
        kern,
        grid=(T // tm,),
        in_specs=[
            pl.BlockSpec((tm, D), row),
            pl.BlockSpec((tm, 1), row),
            pl.BlockSpec((1, D), const),
            pl.BlockSpec((D, in_w), const),
            pl.BlockSpec((1, V7X_LANES), const),
            pl.BlockSpec((sgu_w, V7X_LANES), const),
            pl.BlockSpec((V7X_LANES, sgu_w), const),
            pl.BlockSpec((1, sgu_w), const),
            pl.BlockSpec((1, sgu_w), const),
        ],
        out_specs=[
            pl.BlockSpec((tm, attn_w), row),
            pl.BlockSpec((tm, 4 * V7X_LANES), row),
            pl.BlockSpec((tm, sgu_w), row),
            pl.BlockSpec((tm, sgu_w), row),
        ],
        out_shape=[
            jax.ShapeDtypeStruct((T, attn_w), BF16),
            jax.ShapeDtypeStruct((T, 4 * V7X_LANES), BF16),
            jax.ShapeDtypeStruct((T, sgu_w), F32),
            jax.ShapeDtypeStruct((T, sgu_w), BF16),
        ],
        compiler_params=_params(_vmem_limit(blocks, 0, temps), 1),
        name="in_proj",
    )(h, pos, g_pre, w_in, invf, gsum, gbc, ln_g, ln_b)


def _mix_kernel(sink_ref, q_ref, kv_ref, kvp_ref, u_ref, vn_ref, h_ref, ws_ref, bs_ref,
                ga_ref, gs_ref, wo_ref, gpost_ref, gffn_ref, *rest, moe, attn_w, sgu_w):
    if moe:
        wr_hi_ref, wr_lo_ref, h1_ref, ypk_ref, lg_ref, merged_ref = rest
    else:
        h1_ref, y_ref, merged_ref = rest

    first_tile = pl.program_id(1) == 0
    n_pairs = attn_w // V7X_LANES
    band = 2 * BLOCK

    row = lax.broadcasted_iota(I32, (BLOCK, 2 * band), 0)
    col = lax.broadcasted_iota(I32, (BLOCK, 2 * band), 1) % band
    in_window = (col > row) & (col <= row + WINDOW)
    lane = lax.broadcasted_iota(I32, (1, V7X_LANES), 1)
    low = lane < HEAD_DIM
    tril = (lax.broadcasted_iota(I32, (BLOCK, BLOCK), 0)
            >= lax.broadcasted_iota(I32, (BLOCK, BLOCK), 1))
    zero_bf = jnp.zeros((), BF16)

    for blk in range(BLOCKS_PER_TILE):
        rows = slice(blk * BLOCK, (blk + 1) * BLOCK)
        if blk == 0:
            kv_prev = kvp_ref[...]
            mask = in_window & ((col >= BLOCK) | jnp.logical_not(first_tile))
        else:
            kv_prev = kv_ref[(blk - 1) * BLOCK:blk * BLOCK, :]
            mask = in_window
        kv_band = jnp.concatenate([kv_prev, kv_ref[rows, :]], axis=0)

        attn_parts = []
        for pair in range(n_pairs):
            kvh = (2 * pair) // Q_PER_KV
            kk = kv_band[:, kvh * V7X_LANES:(kvh + 1) * V7X_LANES]
            vv = kv_band[:, (2 + kvh) * V7X_LANES:(3 + kvh) * V7X_LANES]
            k2 = jnp.concatenate([jnp.where(low, kk, zero_bf), jnp.where(low, zero_bf, kk)], axis=0)
            v2 = jnp.concatenate([jnp.where(low, vv, zero_bf), jnp.where(low, zero_bf, vv)], axis=0)
            qp = q_ref[rows, pair * V7X_LANES:(pair + 1) * V7X_LANES]
            s = jnp.where(mask, _dot_nt(qp, k2), NEG_BIG)
            probs, inv = [], []
            for hh in range(2):
                sink = sink_ref[2 * pair + hh]
                sh = s[:, hh * band:(hh + 1) * band]
                m = jnp.maximum(jnp.max(sh, axis=-1, keepdims=True), sink)
                p = jnp.exp(sh - m)
                denom = jnp.sum(p, axis=-1, keepdims=True) + jnp.exp(sink - m)
                probs.append(p.astype(BF16))
                inv.append(1.0 / denom)
            o = _dot(jnp.concatenate(probs, axis=1), v2)
            attn_parts.append(o * jnp.where(low, inv[0], inv[1]))
        attn = jnp.concatenate(attn_parts, axis=1)

        sgu_parts = []
        for c in range(sgu_w // V7X_LANES):
            vn_c = vn_ref[rows, c * V7X_LANES:(c + 1) * V7X_LANES]
            mixed = []
            for hh in range(2):
                g = 2 * c + hh
                w_g = jnp.where(tril, ws_ref[g], zero_bf)
                mixed.append(_dot(w_g, vn_c) + bs_ref[:, g:g + 1])
            sgu_parts.append(jnp.where(low, mixed[0], mixed[1]))
        sgu = u_ref[rows, :] * jnp.concatenate(sgu_parts, axis=1)

        merged_ref[rows, 0:attn_w] = _rmsnorm(attn, ga_ref[...]).astype(BF16)
        merged_ref[rows, attn_w:attn_w + sgu_w] = _rmsnorm(sgu, gs_ref[...]).astype(BF16)

    proj = _dot(merged_ref[...], wo_ref[...])
    h1 = h_ref[...] + _rmsnorm(proj, gpost_ref[...])
    h1_ref[...] = h1
    y = _rmsnorm(h1, gffn_ref[...])
    if not moe:
        y_ref[...] = y.astype(BF16)
    else:
        half = y.shape[1] // 2
        left = lax.bitcast_convert_type(y[:, :half].astype(BF16).astype(F32), U32)
        right = lax.bitcast_convert_type(y[:, half:].astype(BF16).astype(F32), U32)
        ypk_ref[...] = left | (right >> 16)
        y_hi, y_lo = _split_bf16(y)
        wr_hi = wr_hi_ref[...]
        lg_ref[...] = _dot_nt(wr_hi, y_hi) + _dot_nt(wr_hi, y_lo) + _dot_nt(wr_lo_ref[...], y_hi)


def _mix(sinks, q, kv, u, vn, h, w_s, b_s_t, g_attn, g_sgu, w_out, g_post, g_ffn, router, *,
         batch, seq):
    T, D = h.shape
    tq = TOKEN_TILE
    attn_w = q.shape[1]
    sgu_w = u.shape[1]
    tiles_per_seq = seq // tq
    blocks_per_seq = seq // BLOCK
    moe = router is not None
    n_groups = w_s.shape[0]

    row = lambda b, j: (b * tiles_per_seq + j, 0)
    prev = lambda b, j: (b * blocks_per_seq + jnp.maximum(j * BLOCKS_PER_TILE - 1, 0), 0)
    const2 = lambda b, j: (0, 0)
    const3 = lambda b, j: (0, 0, 0)

    in_specs = [
        pl.BlockSpec(memory_space=pltpu.SMEM),
        pl.BlockSpec((tq, attn_w), row),
        pl.BlockSpec((tq, 4 * V7X_LANES), row),
        pl.BlockSpec((BLOCK, 4 * V7X_LANES), prev),
        pl.BlockSpec((tq, sgu_w), row),
        pl.BlockSpec((tq, sgu_w), row),
        pl.BlockSpec((tq, D), row),
        pl.BlockSpec((n_groups, BLOCK, BLOCK), const3),
        pl.BlockSpec((BLOCK, n_groups), const2),
        pl.BlockSpec((1, attn_w), const2),
        pl.BlockSpec((1, sgu_w), const2),
        pl.BlockSpec((D, D), const2),
        pl.BlockSpec((1, D), const2),
        pl.BlockSpec((1, D), const2),
    ]
    args = [sinks, q, kv, kv, u, vn, h, w_s, b_s_t, g_attn, g_sgu, w_out, g_post, g_ffn]
    out_specs = [pl.BlockSpec((tq, D), row)]
    out_shape = [jax.ShapeDtypeStruct((T, D), F32)]
    blocks = (_nbytes((tq, attn_w), BF16) + 2 * _nbytes((tq, 4 * V7X_LANES), BF16)
              + _nbytes((tq, sgu_w), F32) + _nbytes((tq, sgu_w), BF16) + 2 * _nbytes((tq, D), F32)
              + _nbytes((n_groups, BLOCK, BLOCK), BF16) + _nbytes((D, D), BF16)
              + _nbytes((tq, D), F32))
    if moe:
        n_e = router[0].shape[0]
        in_specs += [pl.BlockSpec((n_e, D), const2), pl.BlockSpec((n_e, D), const2)]
        args += list(router)
        out_specs += [pl.BlockSpec((tq, D // 2), row),
                      pl.BlockSpec((n_e, tq), lambda b, j: (0, b * tiles_per_seq + j))]
        out_shape += [jax.ShapeDtypeStruct((T, D // 2), U32),
                      jax.ShapeDtypeStruct((n_e, T), F32)]
    else:
        out_specs += [pl.BlockSpec((tq, D), row)]
        out_shape += [jax.ShapeDtypeStruct((T, D), BF16)]
    scratch = _nbytes((tq, D), BF16)
    temps = 8 * _nbytes((tq, D), F32)
    kern = functools.partial(_mix_kernel, moe=moe, attn_w=attn_w, sgu_w=sgu_w)
    return pl.pallas_call(
        kern,
        grid=(batch, tiles_per_seq),
        in_specs=in_specs,
        out_specs=out_specs,
        out_shape=out_shape,
        scratch_shapes=[pltpu.VMEM((tq, D), BF16)],
        compiler_params=_params(_vmem_limit(blocks, scratch, temps), 2),
        name="mix_moe" if moe else "mix_dense",
    )(*args)


def _dense_ffn_kernel(y_ref, h_ref, wg_ref, wu_ref, wd_ref, g_ref, o_ref, *, chunk):
    y = y_ref[...]
    d_ff = wg_ref.shape[1]
    acc = jnp.zeros(o_ref.shape, F32)
    for c in range(d_ff // chunk):
        sl = slice(c * chunk, (c + 1) * chunk)
        a = _silu(_dot(y, wg_ref[:, sl])) * _dot(y, wu_ref[:, sl])
        acc = acc + _dot(a.astype(BF16), wd_ref[sl, :])
    o_ref[...] = h_ref[...] + _rmsnorm(acc, g_ref[...])


def _dense_ffn(y, h1, w_gate, w_up, w_down, g_post):
    T, D = h1.shape
    d_ff = w_gate.shape[1]
    tm = TOKEN_TILE
    chunk = 2 * V7X_LANES
    assert d_ff % chunk == 0
    row = lambda i: (i, 0)
    const = lambda i: (0, 0)
    blocks = (_nbytes((tm, D), BF16) + 2 * _nbytes((tm, D), F32) + 3 * _nbytes((D, d_ff), BF16))
    temps = 4 * _nbytes((tm, D), F32)
    return pl.pallas_call(
        functools.partial(_dense_ffn_kernel, chunk=chunk),
        grid=(T // tm,),
        in_specs=[
            pl.BlockSpec((tm, D), row),
            pl.BlockSpec((tm, D), row),
            pl.BlockSpec((D, d_ff), const),
            pl.BlockSpec((D, d_ff), const),
            pl.BlockSpec((d_ff, D), const),
            pl.BlockSpec((1, D), const),
        ],
        out_specs=pl.BlockSpec((tm, D), row),
        out_shape=jax.ShapeDtypeStruct((T, D), F32),
        compiler_params=_params(_vmem_limit(blocks, 0, temps), 1),
        name="dense_ffn",
    )(y, h1, w_gate, w_up, w_down, g_post)


def _route_kernel(lg_ref, tri_ref, pos_ref, gate_ref, info_ref, oh_ref, rank_ref):
    n_e, T = lg_ref.shape
    lg = lg_ref[...]
    eidx = lax.broadcasted_iota(I32, (n_e, T), 0)
    m1 = jnp.max(lg, axis=0, keepdims=True)
    i1 = jnp.min(jnp.where(lg == m1, eidx, n_e), axis=0, keepdims=True)
    oh1 = eidx == i1
    lg2 = jnp.where(oh1, -jnp.inf, lg)
    m2 = jnp.max(lg2, axis=0, keepdims=True)
    i2 = jnp.min(jnp.where(lg2 == m2, eidx, n_e), axis=0, keepdims=True)
    oh2 = eidx == i2
    e2 = jnp.exp(m2 - m1)
    gate_ref[0:1, :] = 1.0 / (1.0 + e2)
    gate_ref[1:2, :] = e2 / (1.0 + e2)

    oh_ref[...] = jnp.where(oh1 | oh2, 1.0, 0.0)

    tri = tri_ref[...]

    def body(i, carry):
        start = pl.multiple_of(i * ROUTE_CHUNK, ROUTE_CHUNK)
        ohc = oh_ref[:, pl.ds(start, ROUTE_CHUNK)]
        inc = _dot(ohc.astype(BF16), tri)
        rank_ref[:, pl.ds(start, ROUTE_CHUNK)] = carry + inc - ohc
        return carry + inc[:, ROUTE_CHUNK - 1:ROUTE_CHUNK]

    count = lax.fori_loop(0, T // ROUTE_CHUNK, body, jnp.zeros((n_e, 1), F32))

    tm = np.float32(EXPERT_ROW_TILE)
    n_tiles = jnp.floor((count + (tm - 1.0)) * (1.0 / tm))
    e_col = lax.broadcasted_iota(I32, (n_e, 1), 0)
    first_tile = jnp.zeros((n_e, 1), F32)
    for e in range(n_e - 1):
        nt_e = jnp.sum(jnp.where(e_col == e, n_tiles, 0.0), axis=0, keepdims=True)
        first_tile = first_tile + jnp.where(e_col > e, nt_e, 0.0)
    row0 = first_tile * tm + rank_ref[...]
    pos_ref[0:1, :] = jnp.sum(jnp.where(oh1, row0, 0.0), axis=0, keepdims=True).astype(I32)
    pos_ref[1:2, :] = jnp.sum(jnp.where(oh2, row0, 0.0), axis=0, keepdims=True).astype(I32)

    end_tile = (first_tile + n_tiles).astype(I32)
    tile = lax.broadcasted_iota(I32, (n_e, TILE_INFO_LEN), 1)
    owner = jnp.sum(jnp.where(end_tile <= tile, 1, 0), axis=0, keepdims=True)
    total = jnp.sum(n_tiles, axis=0, keepdims=True).astype(I32)
    lane = lax.broadcasted_iota(I32, (1, TILE_INFO_LEN), 1)
    info_ref[...] = jnp.where(lane == TILE_INFO_LEN - 1, total, jnp.minimum(owner, n_e - 1))


def _route(logits_t, tri):
    n_e, T = logits_t.shape
    blocks = (_nbytes((n_e, T), F32) + _nbytes((ROUTE_CHUNK, ROUTE_CHUNK), BF16)
              + _nbytes((8, T), I32) + _nbytes((8, T), F32))
    scratch = 2 * _nbytes((n_e, T), F32)
    temps = 12 * _nbytes((n_e, T), F32)
    return pl.pallas_call(
        _route_kernel,
        out_shape=[
            jax.ShapeDtypeStruct((TOP_K, T), I32),
            jax.ShapeDtypeStruct((TOP_K, T), F32),
            jax.ShapeDtypeStruct((1, TILE_INFO_LEN), I32),
        ],
        scratch_shapes=[pltpu.VMEM((n_e, T), F32), pltpu.VMEM((n_e, T), F32)],
        compiler_params=pltpu.CompilerParams(vmem_limit_bytes=_vmem_limit(blocks, scratch, temps)),
        name="route",
    )(logits_t, tri)


def _row_copy(src_hbm, src_row, dst_hbm, dst_row, sem):
    return pltpu.make_async_copy(src_hbm.at[pl.ds(src_row, 1)], dst_hbm.at[pl.ds(dst_row, 1)], sem)


def _dispatch_kernel(pos_ref, ypk_hbm, xs_in_hbm, xs_hbm, sem):
    del xs_in_hbm
    tt = pos_ref.shape[1]
    base = pl.program_id(0) * tt

    def issue(t, carry):
        for k in range(TOP_K):
            _row_copy(ypk_hbm, base + t, xs_hbm, pos_ref[k, t], sem).start()
        return carry

    def drain(t, carry):
        for k in range(TOP_K):
            _row_copy(ypk_hbm, 0, xs_hbm, 0, sem).wait()
        return carry

    lax.fori_loop(0, tt, issue, 0, unroll=8)
    lax.fori_loop(0, tt, drain, 0, unroll=8)


def _dispatch(pos, ypk, n_rows):
    T, W = ypk.shape
    tt = DISPATCH_TILE
    xs0 = jnp.zeros((n_rows, W), U32)
    return pl.pallas_call(
        _dispatch_kernel,
        grid=(T // tt,),
        in_specs=[
            pl.BlockSpec((TOP_K, tt), lambda i: (0, i), memory_space=pltpu.SMEM),
            pl.BlockSpec(memory_space=pl.ANY),
            pl.BlockSpec(memory_space=pl.ANY),
        ],
        out_specs=pl.BlockSpec(memory_space=pl.ANY),
        out_shape=jax.ShapeDtypeStruct((n_rows, W), U32),
        scratch_shapes=[pltpu.SemaphoreType.DMA(())],
        input_output_aliases={2: 0},
        compiler_params=pltpu.CompilerParams(dimension_semantics=("arbitrary",),
                                             has_side_effects=True),
        name="dispatch",
    )(pos, ypk, xs0)


def _expert_kernel(info_ref, xs_ref, wg_ref, wu_ref, wd_ref, o_ref):
    i = pl.program_id(0)
    c = pl.program_id(1)
    valid = i < info_ref[TILE_INFO_LEN - 1]

    @pl.when(c == 0)
    def _():
        o_ref[...] = jnp.zeros(o_ref.shape, F32)

    @pl.when(valid)
    def _():
        xp = xs_ref[...]
        left = lax.bitcast_convert_type(xp & np.uint32(0xFFFF0000), F32).astype(BF16)
        right = lax.bitcast_convert_type(xp << 16, F32).astype(BF16)
        x = jnp.concatenate([left, right], axis=1)
        a = _silu(_dot(x, wg_ref[0])) * _dot(x, wu_ref[0])
        o_ref[...] += _dot(a.astype(BF16), wd_ref[0])


def _expert_ffn(tile_info, xs, w_gate, w_up, w_down, n_chunks):
    R, W = xs.shape
    n_e, D, d_ff = w_gate.shape
    tm = EXPERT_ROW_TILE
    ck = d_ff // n_chunks
    last = TILE_INFO_LEN - 1

    def tile_of(i, info):
        return jnp.minimum(i, info[last] - 1)

    def chunk_of(i, c, info):
        return jnp.where(i < info[last], c, n_chunks - 1)

    grid_spec = pltpu.PrefetchScalarGridSpec(
        num_scalar_prefetch=1,
        grid=(R // tm, n_chunks),
        in_specs=[
            pl.BlockSpec((tm, W), lambda i, c, info: (tile_of(i, info), 0)),
            pl.BlockSpec((1, D, ck), lambda i, c, info: (info[tile_of(i, info)], 0, chunk_of(i, c, info))),
            pl.BlockSpec((1, D, ck), lambda i, c, info: (info[tile_of(i, info)], 0, chunk_of(i, c, info))),
            pl.BlockSpec((1, ck, D), lambda i, c, info: (info[tile_of(i, info)], chunk_of(i, c, info), 0)),
        ],
        out_specs=pl.BlockSpec((tm, D), lambda i, c, info: (i, 0)),
    )
    blocks = (_nbytes((tm, W), U32) + 3 * _nbytes((D, ck), BF16) + _nbytes((tm, D), F32))
    temps = 3 * _nbytes((tm, ck), F32) + 2 * _nbytes((tm, D), F32)
    return pl.pallas_call(
        _expert_kernel,
        grid_spec=grid_spec,
        out_shape=jax.ShapeDtypeStruct((R, D), F32),
        compiler_params=_params(_vmem_limit(blocks, 0, temps), 2),
        name="expert_ffn",
    )(tile_info, xs, w_gate, w_up, w_down)


def _combine_kernel(pos_ref, gate_ref, h_ref, g_ref, os_hbm, o_ref, a_ref, b_ref, sem):
    tt = h_ref.shape[0]
    bufs = (a_ref, b_ref)

    def issue(t, carry):
        for k in range(TOP_K):
            _row_copy(os_hbm, pos_ref[k, t], bufs[k], t, sem).start()
        return carry

    def drain(t, carry):
        for k in range(TOP_K):
            _row_copy(os_hbm, 0, bufs[k], 0, sem).wait()
        return carry

    lax.fori_loop(0, tt, issue, 0, unroll=8)
    lax.fori_loop(0, tt, drain, 0, unroll=8)
    f = gate_ref[:, 0:1] * a_ref[...] + gate_ref[:, 1:2] * b_ref[...]
    o_ref[...] = h_ref[...] + _rmsnorm(f, g_ref[...])


def _combine(pos, gate_t, h1, g_post, o_sorted):
    T, D = h1.shape
    tt = COMBINE_TILE
    row = lambda i: (i, 0)
    blocks = (_nbytes((tt, V7X_LANES), F32) + 2 * _nbytes((tt, D), F32))
    scratch = 2 * _nbytes((tt, D), F32)
    temps = 3 * _nbytes((tt, D), F32)
    return pl.pallas_call(
        _combine_kernel,
        grid=(T // tt,),
        in_specs=[
            pl.BlockSpec((TOP_K, tt), lambda i: (0, i), memory_space=pltpu.SMEM),
            pl.BlockSpec((tt, TOP_K), row),
            pl.BlockSpec((tt, D), row),
            pl.BlockSpec((1, D), lambda i: (0, 0)),
            pl.BlockSpec(memory_space=pl.ANY),
        ],
        out_specs=pl.BlockSpec((tt, D), row),
        out_shape=jax.ShapeDtypeStruct((T, D), F32),
        scratch_shapes=[pltpu.VMEM((tt, D), F32), pltpu.VMEM((tt, D), F32),
                        pltpu.SemaphoreType.DMA(())],
        compiler_params=_params(_vmem_limit(blocks, scratch, temps), 1),
        name="combine",
    )(pos, gate_t, h1, g_post, o_sorted)


def _moe_ffn(ypk, logits_t, h1, g_post, w_gate, w_up, w_down, tri):
    T = h1.shape[0]
    n_tiles_max = (TOP_K * T) // EXPERT_ROW_TILE + N_EXPERTS
    assert n_tiles_max < TILE_INFO_LEN
    pos, gate, info = _route(logits_t, tri)
    xs = _dispatch(pos, ypk, n_tiles_max * EXPERT_ROW_TILE)
    o_sorted = _expert_ffn(info.reshape(TILE_INFO_LEN), xs, w_gate, w_up, w_down, n_chunks=2)
    return _combine(pos, gate.T, h1, g_post, o_sorted)


def _rope_inv_freq_lanes():
    lane = np.arange(V7X_LANES) % HEAD_DIM
    half = ROPE_DIM // 2
    idx = np.where(lane < half, lane, lane - half)
    inv = np.power(np.float32(ROPE_THETA), -(2.0 * idx.astype(np.float32)) / np.float32(ROPE_DIM))
    return jnp.asarray(np.where(lane < ROPE_DIM, inv, 0.0).astype(np.float32)).reshape(1, V7X_LANES)


def _group_tables(sgu_w):
    grp = np.arange(sgu_w) // HEAD_DIM
    gsum = (grp[:, None] == np.arange(V7X_LANES)[None, :]).astype(np.float32)
    return jnp.asarray(gsum, BF16), jnp.asarray(gsum.T, BF16)


def kernel(x, positions, pre_mix_norm, w_in, attn_sinks, sgu_ln_g, sgu_ln_b, sgu_w, sgu_b, attn_out_norm, sgu_out_norm, w_out, post_mix_norm, pre_ffn_norm, post_ffn_norm, ffn_w_gate, ffn_w_up, ffn_w_down, moe_router, moe_w_gate, moe_w_up, moe_w_down):
    B, S, D = x.shape
    T = B * S
    depth = pre_mix_norm.shape[0]
    n_q_heads = attn_sinks.shape[1]
    attn_w = n_q_heads * HEAD_DIM
    kv_w = (n_q_heads // Q_PER_KV) * HEAD_DIM
    sgu_width = sgu_ln_g.shape[1]
    assert kv_w == V7X_LANES and attn_w % V7X_LANES == 0 and sgu_width % V7X_LANES == 0
    assert S % TOKEN_TILE == 0 and T % DISPATCH_TILE == 0 and sgu_w.shape[2] == BLOCK
    assert moe_router.shape[-1] == N_EXPERTS

    invf = _rope_inv_freq_lanes()
    gsum, gbc = _group_tables(sgu_width)
    tri = jnp.asarray(np.triu(np.ones((ROUTE_CHUNK, ROUTE_CHUNK), np.float32)), BF16)
    row2 = lambda a: a.reshape(1, -1)

    h = x.reshape(T, D)
    pos = positions.reshape(T, 1)
    for layer in range(depth):
        q, kv, u, vn = _in_proj(
            h, pos, row2(pre_mix_norm[layer]), w_in[layer].astype(BF16), invf, gsum, gbc,
            row2(sgu_ln_g[layer]), row2(sgu_ln_b[layer]), attn_w=attn_w, kv_w=kv_w, sgu_w=sgu_width)
        is_moe = layer % 2 == 1
        i = layer // 2
        router = None
        if is_moe:
            wr = moe_router[i].T
            wr_hi = wr.astype(BF16)
            router = (wr_hi, (wr - wr_hi.astype(F32)).astype(BF16))
        outs = _mix(
            attn_sinks[layer], q, kv, u, vn, h, sgu_w[layer].astype(BF16), sgu_b[layer].T,
            row2(attn_out_norm[layer]), row2(sgu_out_norm[layer]), w_out[layer].astype(BF16),
            row2(post_mix_norm[layer]), row2(pre_ffn_norm[layer]), router, batch=B, seq=S)
        g_post = row2(post_ffn_norm[layer])
        if is_moe:
            h1, ypk, logits_t = outs
            h = _moe_ffn(ypk, logits_t, h1, g_post, moe_w_gate[i].astype(BF16),
                         moe_w_up[i].astype(BF16), moe_w_down[i].astype(BF16), tri)
        else:
            h1, y = outs
            h = _dense_ffn(y, h1, ffn_w_gate[i].astype(BF16), ffn_w_up[i].astype(BF16),
                           ffn_w_down[i].astype(BF16), g_post)
    return h.reshape(B, S, D)
```

```python
import functools

import numpy as np
import jax
import jax.numpy as jnp
from jax import lax
from jax.experimental import pallas as pl
from jax.experimental.pallas import tpu as pltpu

F32 = jnp.float32
BF16 = jnp.bfloat16
I32 = jnp.int32
U32 = jnp.uint32

V7X_LANES = 128
V7X_SUBLANES = 8
V7X_VMEM_BUDGET_BYTES = 56 * 1024 * 1024

HEAD_DIM = 64
Q_PER_KV = 4
WINDOW = 128
BLOCK = 128
ROPE_DIM = HEAD_DIM // 4
ROPE_THETA = 500000.0
N_EXPERTS = 8
TOP_K = 2
EPS = 1e-6
NEG_BIG = -1e30

TOKEN_TILE = 512
BLOCKS_PER_TILE = TOKEN_TILE // BLOCK
EXPERT_ROW_TILE = 512
ROUTE_CHUNK = 512
DISPATCH_TILE = 1024
COMBINE_TILE = 256
TILE_INFO_LEN = 256


def _vmem_limit(block_bytes, scratch_bytes=0, temp_bytes=0):
    need = 2 * int(block_bytes) + int(scratch_bytes) + int(temp_bytes)
    return int(min(max(need, 16 * 1024 * 1024), V7X_VMEM_BUDGET_BYTES))


def _nbytes(shape, dtype):
    return int(np.prod(shape)) * jnp.dtype(dtype).itemsize


def _params(vmem_bytes, n_grid_dims):
    return pltpu.CompilerParams(
        dimension_semantics=("arbitrary",) * n_grid_dims,
        vmem_limit_bytes=vmem_bytes,
    )


def _rmsnorm(x, g):
    return x * lax.rsqrt(jnp.mean(x * x, axis=-1, keepdims=True) + EPS) * g


def _gelu_tanh(x):
    c = np.float32(np.sqrt(2.0 / np.pi))
    return x * (0.5 * (1.0 + jnp.tanh(c * (x + 0.044715 * (x * x * x)))))


def _silu(x):
    return x * (1.0 / (1.0 + jnp.exp(-x)))


def _split_bf16(x):
    hi = x.astype(BF16)
    lo = (x - hi.astype(F32)).astype(BF16)
    return hi, lo


def _dot(a, b):
    return jnp.dot(a, b, preferred_element_type=F32)


def _dot_nt(a, b):
    return lax.dot_general(a, b, (((1,), (1,)), ((), ())), preferred_element_type=F32)


def _in_proj_kernel(h_ref, pos_ref, g_ref, w_ref, invf_ref, gsum_ref, gbc_ref, lng_ref, lnb_ref,
                    q_ref, kv_ref, u_ref, vn_ref, *, attn_w, kv_w, sgu_w):
    hn = _rmsnorm(h_ref[...], g_ref[...]).astype(BF16)
    z = _dot(hn, w_ref[...])

    ang = pos_ref[...].astype(F32) * invf_ref[...]
    lane = lax.broadcasted_iota(I32, (1, V7X_LANES), 1)
    hl = lane % HEAD_DIM
    cos = jnp.cos(ang)
    sin = jnp.sin(ang)
    half = ROPE_DIM // 2
    sin_a = jnp.where(hl < half, -sin, 0.0)
    sin_b = jnp.where((hl >= half) & (hl < ROPE_DIM), sin, 0.0)

    def rope(t):
        return (t * cos + pltpu.roll(t, V7X_LANES - half, 1) * sin_a
                + pltpu.roll(t, half, 1) * sin_b)

    scale = np.float32(HEAD_DIM ** -0.5)
    for c in range(attn_w // V7X_LANES):
        sl = slice(c * V7X_LANES, (c + 1) * V7X_LANES)
        q_ref[:, sl] = (rope(z[:, sl]) * scale).astype(BF16)

    low = lane < HEAD_DIM
    k = rope(z[:, attn_w:attn_w + kv_w])
    v = z[:, attn_w + kv_w:attn_w + 2 * kv_w]
    kr = pltpu.roll(k, HEAD_DIM, 1)
    vr = pltpu.roll(v, HEAD_DIM, 1)
    kv_ref[:, 0 * V7X_LANES:1 * V7X_LANES] = jnp.where(low, k, kr).astype(BF16)
    kv_ref[:, 1 * V7X_LANES:2 * V7X_LANES] = jnp.where(low, kr, k).astype(BF16)
    kv_ref[:, 2 * V7X_LANES:3 * V7X_LANES] = jnp.where(low, v, vr).astype(BF16)
    kv_ref[:, 3 * V7X_LANES:4 * V7X_LANES] = jnp.where(low, vr, v).astype(BF16)

    o = attn_w + 2 * kv_w
    u_ref[...] = _gelu_tanh(z[:, o:o + sgu_w])

    zv = _gelu_tanh(z[:, o + sgu_w:o + 2 * sgu_w])
    gsum = gsum_ref[...]
    gbc = gbc_ref[...]

    def group_mean(a):
        hi, lo = _split_bf16(a)
        s = (_dot(hi, gsum) + _dot(lo, gsum)) * np.float32(1.0 / HEAD_DIM)
        hi, lo = _split_bf16(s)
        return _dot(hi, gbc) + _dot(lo, gbc)

    d = zv - group_mean(zv)
    var = group_mean(d * d)
    vn_ref[...] = (d * lax.rsqrt(var + EPS) * lng_ref[...] + lnb_ref[...]).astype(BF16)


def _in_proj(h, pos, g_pre, w_in, invf, gsum, gbc, ln_g, ln_b, *, attn_w, kv_w, sgu_w):
    T, D = h.shape
    tm = TOKEN_TILE
    in_w = w_in.shape[1]
    row = lambda i: (i, 0)
    const = lambda i: (0, 0)
    blocks = (_nbytes((tm, D), F32) + _nbytes((tm, V7X_LANES), I32) + _nbytes((D, in_w), BF16)
              + _nbytes((tm, attn_w), BF16) + _nbytes((tm, 4 * V7X_LANES), BF16)
              + _nbytes((tm, sgu_w), F32) + _nbytes((tm, sgu_w), BF16))
    temps = 6 * _nbytes((tm, in_w), F32)
    kern = functools.partial(_in_proj_kernel, attn_w=attn_w, kv_w=kv_w, sgu_w=sgu_w)
    return pl.pallas_call(
        kern,
        grid=(T // tm,),
        in_specs=[
            pl.BlockSpec((tm, D), row),
            pl.BlockSpec((tm, 1), row),
            pl.BlockSpec((1, D), const),
            pl.BlockSpec((D, in_w), const),
            pl.BlockSpec((1, V7X_LANES), const),
            pl.BlockSpec((sgu_w, V7X_LANES), const),
            pl.BlockSpec((V7X_LANES, sgu_w), const),
            pl.BlockSpec((1, sgu_w), const),
            pl.BlockSpec((1, sgu_w), const),
        ],
        out_specs=[
            pl.BlockSpec((tm, attn_w), row),
            pl.BlockSpec((tm, 4 * V7X_LANES), row),
            pl.BlockSpec((tm, sgu_w), row),
            pl.BlockSpec((tm, sgu_w), row),
        ],
        out_shape=[
            jax.ShapeDtypeStruct((T, attn_w), BF16),
            jax.ShapeDtypeStruct((T, 4 * V7X_LANES), BF16),
            jax.ShapeDtypeStruct((T, sgu_w), F32),
            jax.ShapeDtypeStruct((T, sgu_w), BF16),
        ],
        compiler_params=_params(_vmem_limit(blocks, 0, temps), 1),
        name="in_proj",
    )(h, pos, g_pre, w_in, invf, gsum, gbc, ln_g, ln_b)


def _mix_kernel(sink_ref, q_ref, kv_ref, kvp_ref, u_ref, vn_ref, h_ref, ws_ref, bs_ref,
                ga_ref, gs_ref, wo_ref, gpost_ref, gffn_ref, *rest, moe, attn_w, sgu_w):
    if moe:
        wr_hi_ref, wr_lo_ref, h1_ref, ypk_ref, lg_ref, merged_ref = rest
    else:
        h1_ref, y_ref, merged_ref = rest

    first_tile = pl.program_id(1) == 0
    n_pairs = attn_w // V7X_LANES
    band = 2 * BLOCK

    row = lax.broadcasted_iota(I32, (BLOCK, 2 * band), 0)
    col = lax.broadcasted_iota(I32, (BLOCK, 2 * band), 1) % band
    in_window = (col > row) & (col <= row + WINDOW)
    lane = lax.broadcasted_iota(I32, (1, V7X_LANES), 1)
    low = lane < HEAD_DIM
    tril = (lax.broadcasted_iota(I32, (BLOCK, BLOCK), 0)
            >= lax.broadcasted_iota(I32, (BLOCK, BLOCK), 1))
    zero_bf = jnp.zeros((), BF16)

    for blk in range(BLOCKS_PER_TILE):
        rows = slice(blk * BLOCK, (blk + 1) * BLOCK)
        if blk == 0:
            kv_prev = kvp_ref[...]
            mask = in_window & ((col >= BLOCK) | jnp.logical_not(first_tile))
        else:
            kv_prev = kv_ref[(blk - 1) * BLOCK:blk * BLOCK, :]
            mask = in_window
        kv_band = jnp.concatenate([kv_prev, kv_ref[rows, :]], axis=0)

        attn_parts = []
        for pair in range(n_pairs):
            kvh = (2 * pair) // Q_PER_KV
            kk = kv_band[:, kvh * V7X_LANES:(kvh + 1) * V7X_LANES]
            vv = kv_band[:, (2 + kvh) * V7X_LANES:(3 + kvh) * V7X_LANES]
            k2 = jnp.concatenate([jnp.where(low, kk, zero_bf), jnp.where(low, zero_bf, kk)], axis=0)
            v2 = jnp.concatenate([jnp.where(low, vv, zero_bf), jnp.where(low, zero_bf, vv)], axis=0)
            qp = q_ref[rows, pair * V7X_LANES:(pair + 1) * V7X_LANES]
            s = jnp.where(mask, _dot_nt(qp, k2), NEG_BIG)
            probs, inv = [], []
            for hh in range(2):
                sink = sink_ref[2 * pair + hh]
                sh = s[:, hh * band:(hh + 1) * band]
                m = jnp.maximum(jnp.max(sh, axis=-1, keepdims=True), sink)
                p = jnp.exp(sh - m)
                denom = jnp.sum(p, axis=-1, keepdims=True) + jnp.exp(sink - m)
                probs.append(p.astype(BF16))
                inv.append(1.0 / denom)
            o = _dot(jnp.concatenate(probs, axis=1), v2)
            attn_parts.append(o * jnp.where(low, inv[0], inv[1]))
        attn = jnp.concatenate(attn_parts, axis=1)

        sgu_parts = []
        for c in range(sgu_w // V7X_LANES):
            vn_c = vn_ref[rows, c * V7X_LANES:(c + 1) * V7X_LANES]
            mixed = []
            for hh in range(2):
                g = 2 * c + hh
                w_g = jnp.where(tril, ws_ref[g], zero_bf)
                mixed.append(_dot(w_g, vn_c) + bs_ref[:, g:g + 1])
            sgu_parts.append(jnp.where(low, mixed[0], mixed[1]))
        sgu = u_ref[rows, :] * jnp.concatenate(sgu_parts, axis=1)

        merged_ref[rows, 0:attn_w] = _rmsnorm(attn, ga_ref[...]).astype(BF16)
        merged_ref[rows, attn_w:attn_w + sgu_w] = _rmsnorm(sgu, gs_ref[...]).astype(BF16)

    proj = _dot(merged_ref[...], wo_ref[...])
    h1 = h_ref[...] + _rmsnorm(proj, gpost_ref[...])
    h1_ref[...] = h1
    y = _rmsnorm(h1, gffn_ref[...])
    if not moe:
        y_ref[...] = y.astype(BF16)
    else:
        half = y.shape[1] // 2
        left = lax.bitcast_convert_type(y[:, :half].astype(BF16).astype(F32), U32)
        right = lax.bitcast_convert_type(y[:, half:].astype(BF16).astype(F32), U32)
        ypk_ref[...] = left | (right >> 16)
        y_hi, y_lo = _split_bf16(y)
        wr_hi = wr_hi_ref[...]
        lg_ref[...] = _dot_nt(wr_hi, y_hi) + _dot_nt(wr_hi, y_lo) + _dot_nt(wr_lo_ref[...], y_hi)


def _mix(sinks, q, kv, u, vn, h, w_s, b_s_t, g_attn, g_sgu, w_out, g_post, g_ffn, router, *,
         batch, seq):
    T, D = h.shape
    tq = TOKEN_TILE
    attn_w = q.shape[1]
    sgu_w = u.shape[1]
    tiles_per_seq = seq // tq
    blocks_per_seq = seq // BLOCK
    moe = router is not None
    n_groups = w_s.shape[0]

    row = lambda b, j: (b * tiles_per_seq + j, 0)
    prev = lambda b, j: (b * blocks_per_seq + jnp.maximum(j * BLOCKS_PER_TILE - 1, 0), 0)
    const2 = lambda b, j: (0, 0)
    const3 = lambda b, j: (0, 0, 0)

    in_specs = [
        pl.BlockSpec(memory_space=pltpu.SMEM),
        pl.BlockSpec((tq, attn_w), row),
        pl.BlockSpec((tq, 4 * V7X_LANES), row),
        pl.BlockSpec((BLOCK, 4 * V7X_LANES), prev),
        pl.BlockSpec((tq, sgu_w), row),
        pl.BlockSpec((tq, sgu_w), row),
        pl.BlockSpec((tq, D), row),
        pl.BlockSpec((n_groups, BLOCK, BLOCK), const3),
        pl.BlockSpec((BLOCK, n_groups), const2),
        pl.BlockSpec((1, attn_w), const2),
        pl.BlockSpec((1, sgu_w), const2),
        pl.BlockSpec((D, D), const2),
        pl.BlockSpec((1, D), const2),
        pl.BlockSpec((1, D), const2),
    ]
    args = [sinks, q, kv, kv, u, vn, h, w_s, b_s_t, g_attn, g_sgu, w_out, g_post, g_ffn]
    out_specs = [pl.BlockSpec((tq, D), row)]
    out_shape = [jax.ShapeDtypeStruct((T, D), F32)]
    blocks = (_nbytes((tq, attn_w), BF16) + 2 * _nbytes((tq, 4 * V7X_LANES), BF16)
              + _nbytes((tq, sgu_w), F32) + _nbytes((tq, sgu_w), BF16) + 2 * _nbytes((tq, D), F32)
              + _nbytes((n_groups, BLOCK, BLOCK), BF16) + _nbytes((D, D), BF16)
              + _nbytes((tq, D), F32))
    if moe:
        n_e = router[0].shape[0]
        in_specs += [pl.BlockSpec((n_e, D), const2), pl.BlockSpec((n_e, D), const2)]
        args += list(router)
        out_specs += [pl.BlockSpec((tq, D // 2), row),
                      pl.BlockSpec((n_e, tq), lambda b, j: (0, b * tiles_per_seq + j))]
        out_shape += [jax.ShapeDtypeStruct((T, D // 2), U32),
                      jax.ShapeDtypeStruct((n_e, T), F32)]
    else:
        out_specs += [pl.BlockSpec((tq, D), row)]
        out_shape += [jax.ShapeDtypeStruct((T, D), BF16)]
    scratch = _nbytes((tq, D), BF16)
    temps = 8 * _nbytes((tq, D), F32)
    kern = functools.partial(_mix_kernel, moe=moe, attn_w=attn_w, sgu_w=sgu_w)
    return pl.pallas_call(
        kern,
        grid=(batch, tiles_per_seq),
        in_specs=in_specs,
        out_specs=out_specs,
        out_shape=out_shape,
        scratch_shapes=[pltpu.VMEM((tq, D), BF16)],
        compiler_params=_params(_vmem_limit(blocks, scratch, temps), 2),
        name="mix_moe" if moe else "mix_dense",
    )(*args)


def _dense_ffn_kernel(y_ref, h_ref, wg_ref, wu_ref, wd_ref, g_ref, o_ref, *, chunk):
    y = y_ref[...]
    d_ff = wg_ref.shape[1]
    acc = jnp.zeros(o_ref.shape, F32)
    for c in range(d_ff // chunk):
        sl = slice(c * chunk, (c + 1) * chunk)
        a = _silu(_dot(y, wg_ref[:, sl])) * _dot(y, wu_ref[:, sl])
        acc = acc + _dot(a.astype(BF16), wd_ref[sl, :])
    o_ref[...] = h_ref[...] + _rmsnorm(acc, g_ref[...])


def _dense_ffn(y, h1, w_gate, w_up, w_down, g_post):
    T, D = h1.shape
    d_ff = w_gate.shape[1]
    tm = TOKEN_TILE
    chunk = 2 * V7X_LANES
    assert d_ff % chunk == 0
    row = lambda i: (i, 0)
    const = lambda i: (0, 0)
    blocks = (_nbytes((tm, D), BF16) + 2 * _nbytes((tm, D), F32) + 3 * _nbytes((D, d_ff), BF16))
    temps = 4 * _nbytes((tm, D), F32)
    return pl.pallas_call(
        functools.partial(_dense_ffn_kernel, chunk=chunk),
        grid=(T // tm,),
        in_specs=[
            pl.BlockSpec((tm, D), row),
            pl.BlockSpec((tm, D), row),
            pl.BlockSpec((D, d_ff), const),
            pl.BlockSpec((D, d_ff), const),
            pl.BlockSpec((d_ff, D), const),
            pl.BlockSpec((1, D), const),
        ],
        out_specs=pl.BlockSpec((tm, D), row),
        out_shape=jax.ShapeDtypeStruct((T, D), F32),
        compiler_params=_params(_vmem_limit(blocks, 0, temps), 1),
        name="dense_ffn",
    )(y, h1, w_gate, w_up, w_down, g_post)


def _route_kernel(lg_ref, tri_ref, pos_ref, gate_ref, info_ref, oh_ref, rank_ref):
    n_e, T = lg_ref.shape
    lg = lg_ref[...]
    eidx = lax.broadcasted_iota(I32, (n_e, T), 0)
    m1 = jnp.max(lg, axis=0, keepdims=True)
    i1 = jnp.min(jnp.where(lg == m1, eidx, n_e), axis=0, keepdims=True)
    oh1 = eidx == i1
    lg2 = jnp.where(oh1, -jnp.inf, lg)
    m2 = jnp.max(lg2, axis=0, keepdims=True)
    i2 = jnp.min(jnp.where(lg2 == m2, eidx, n_e), axis=0, keepdims=True)
    oh2 = eidx == i2
    e2 = jnp.exp(m2 - m1)
    gate_ref[0:1, :] = 1.0 / (1.0 + e2)
    gate_ref[1:2, :] = e2 / (1.0 + e2)

    oh_ref[...] = jnp.where(oh1 | oh2, 1.0, 0.0)

    tri = tri_ref[...]

    def body(i, carry):
        start = pl.multiple_of(i * ROUTE_CHUNK, ROUTE_CHUNK)
        ohc = oh_ref[:, pl.ds(start, ROUTE_CHUNK)]
        inc = _dot(ohc.astype(BF16), tri)
        rank_ref[:, pl.ds(start, ROUTE_CHUNK)] = carry + inc - ohc
        return carry + inc[:, ROUTE_CHUNK - 1:ROUTE_CHUNK]

    count = lax.fori_loop(0, T // ROUTE_CHUNK, body, jnp.zeros((n_e, 1), F32))

    tm = np.float32(EXPERT_ROW_TILE)
    n_tiles = jnp.floor((count + (tm - 1.0)) * (1.0 / tm))
    e_col = lax.broadcasted_iota(I32, (n_e, 1), 0)
    first_tile = jnp.zeros((n_e, 1), F32)
    for e in range(n_e - 1):
        nt_e = jnp.sum(jnp.where(e_col == e, n_tiles, 0.0), axis=0, keepdims=True)
        first_tile = first_tile + jnp.where(e_col > e, nt_e, 0.0)
    row0 = first_tile * tm + rank_ref[...]
    pos_ref[0:1, :] = jnp.sum(jnp.where(oh1, row0, 0.0), axis=0, keepdims=True).astype(I32)
    pos_ref[1:2, :] = jnp.sum(jnp.where(oh2, row0, 0.0), axis=0, keepdims=True).astype(I32)

    end_tile = (first_tile + n_tiles).astype(I32)
    tile = lax.broadcasted_iota(I32, (n_e, TILE_INFO_LEN), 1)
    owner = jnp.sum(jnp.where(end_tile <= tile, 1, 0), axis=0, keepdims=True)
    total = jnp.sum(n_tiles, axis=0, keepdims=True).astype(I32)
    lane = lax.broadcasted_iota(I32, (1, TILE_INFO_LEN), 1)
    info_ref[...] = jnp.where(lane == TILE_INFO_LEN - 1, total, jnp.minimum(owner, n_e - 1))


def _route(logits_t, tri):
    n_e, T = logits_t.shape
    blocks = (_nbytes((n_e, T), F32) + _nbytes((ROUTE_CHUNK, ROUTE_CHUNK), BF16)
              + _nbytes((8, T), I32) + _nbytes((8, T), F32))
    scratch = 2 * _nbytes((n_e, T), F32)
    temps = 12 * _nbytes((n_e, T), F32)
    return pl.pallas_call(
        _route_kernel,
        out_shape=[
            jax.ShapeDtypeStruct((TOP_K, T), I32),
            jax.ShapeDtypeStruct((TOP_K, T), F32),
            jax.ShapeDtypeStruct((1, TILE_INFO_LEN), I32),
        ],
        scratch_shapes=[pltpu.VMEM((n_e, T), F32), pltpu.VMEM((n_e, T), F32)],
        compiler_params=pltpu.CompilerParams(vmem_limit_bytes=_vmem_limit(blocks, scratch, temps)),
        name="route",
    )(logits_t, tri)


def _row_copy(src_hbm, src_row, dst_hbm, dst_row, sem):
    return pltpu.make_async_copy(src_hbm.at[pl.ds(src_row, 1)], dst_hbm.at[pl.ds(dst_row, 1)], sem)


def _dispatch_kernel(pos_ref, ypk_ref, xs_in_hbm, xs_hbm, sem):
    del xs_in_hbm
    tt = pos_ref.shape[1]

    def issue(t, carry):
        for k in range(TOP_K):
            _row_copy(ypk_ref, t, xs_hbm, pos_ref[k, t], sem).start(priority=k)
        return carry

    def drain(t, carry):
        for k in range(TOP_K):
            _row_copy(ypk_ref, 0, xs_hbm, 0, sem).wait()
        return carry

    lax.fori_loop(0, tt, issue, 0, unroll=8)
    lax.fori_loop(0, tt, drain, 0, unroll=8)


def _dispatch(pos, ypk, n_rows):
    T, W = ypk.shape
    tt = DISPATCH_TILE
    xs0 = jnp.zeros((n_rows, W), U32)
    return pl.pallas_call(
        _dispatch_kernel,
        grid=(T // tt,),
        in_specs=[
            pl.BlockSpec((TOP_K, tt), lambda i: (0, i), memory_space=pltpu.SMEM),
            pl.BlockSpec((tt, W), lambda i: (i, 0)),
            pl.BlockSpec(memory_space=pl.ANY),
        ],
        out_specs=pl.BlockSpec(memory_space=pl.ANY),
        out_shape=jax.ShapeDtypeStruct((n_rows, W), U32),
        scratch_shapes=[pltpu.SemaphoreType.DMA(())],
        input_output_aliases={2: 0},
        compiler_params=pltpu.CompilerParams(
            dimension_semantics=("arbitrary",), has_side_effects=True,
            vmem_limit_bytes=_vmem_limit(_nbytes((tt, W), U32))),
        name="dispatch",
    )(pos, ypk, xs0)


def _expert_kernel(info_ref, xs_ref, wg_ref, wu_ref, wd_ref, o_ref):
    i = pl.program_id(0)
    c = pl.program_id(1)
    valid = i < info_ref[TILE_INFO_LEN - 1]

    @pl.when(c == 0)
    def _():
        o_ref[...] = jnp.zeros(o_ref.shape, F32)

    @pl.when(valid)
    def _():
        xp = xs_ref[...]
        left = lax.bitcast_convert_type(xp & np.uint32(0xFFFF0000), F32).astype(BF16)
        right = lax.bitcast_convert_type(xp << 16, F32).astype(BF16)
        x = jnp.concatenate([left, right], axis=1)
        a = _silu(_dot(x, wg_ref[0])) * _dot(x, wu_ref[0])
        o_ref[...] += _dot(a.astype(BF16), wd_ref[0])


def _expert_ffn(tile_info, xs, w_gate, w_up, w_down, n_chunks):
    R, W = xs.shape
    n_e, D, d_ff = w_gate.shape
    tm = EXPERT_ROW_TILE
    ck = d_ff // n_chunks
    last = TILE_INFO_LEN - 1

    def tile_of(i, info):
        return jnp.minimum(i, info[last] - 1)

    def chunk_of(i, c, info):
        return jnp.where(i < info[last], c, n_chunks - 1)

    grid_spec = pltpu.PrefetchScalarGridSpec(
        num_scalar_prefetch=1,
        grid=(R // tm, n_chunks),
        in_specs=[
            pl.BlockSpec((tm, W), lambda i, c, info: (tile_of(i, info), 0)),
            pl.BlockSpec((1, D, ck), lambda i, c, info: (info[tile_of(i, info)], 0, chunk_of(i, c, info))),
            pl.BlockSpec((1, D, ck), lambda i, c, info: (info[tile_of(i, info)], 0, chunk_of(i, c, info))),
            pl.BlockSpec((1, ck, D), lambda i, c, info: (info[tile_of(i, info)], chunk_of(i, c, info), 0)),
        ],
        out_specs=pl.BlockSpec((tm, D), lambda i, c, info: (i, 0)),
    )
    blocks = (_nbytes((tm, W), U32) + 3 * _nbytes((D, ck), BF16) + _nbytes((tm, D), F32))
    temps = 3 * _nbytes((tm, ck), F32) + 2 * _nbytes((tm, D), F32)
    return pl.pallas_call(
        _expert_kernel,
        grid_spec=grid_spec,
        out_shape=jax.ShapeDtypeStruct((R, D), F32),
        compiler_params=_params(_vmem_limit(blocks, 0, temps), 2),
        name="expert_ffn",
    )(tile_info, xs, w_gate, w_up, w_down)


def _combine_kernel(pos_ref, gate_ref, h_ref, g_ref, os_hbm, o_ref, a_ref, b_ref, sem):
    tt = h_ref.shape[0]
    bufs = (a_ref, b_ref)

    def issue(t, carry):
        for k in range(TOP_K):
            _row_copy(os_hbm, pos_ref[k, t], bufs[k], t, sem).start(priority=k)
        return carry

    def drain(t, carry):
        for k in range(TOP_K):
            _row_copy(os_hbm, 0, bufs[k], 0, sem).wait()
        return carry

    lax.fori_loop(0, tt, issue, 0, unroll=8)
    lax.fori_loop(0, tt, drain, 0, unroll=8)
    f = gate_ref[:, 0:1] * a_ref[...] + gate_ref[:, 1:2] * b_ref[...]
    o_ref[...] = h_ref[...] + _rmsnorm(f, g_ref[...])


def _combine(pos, gate_t, h1, g_post, o_sorted):
    T, D = h1.shape
    tt = COMBINE_TILE
    row = lambda i: (i, 0)
    blocks = (_nbytes((tt, V7X_LANES), F32) + 2 * _nbytes((tt, D), F32))
    scratch = 2 * _nbytes((tt, D), F32)
    temps = 3 * _nbytes((tt, D), F32)
    return pl.pallas_call(
        _combine_kernel,
        grid=(T // tt,),
        in_specs=[
            pl.BlockSpec((TOP_K, tt), lambda i: (0, i), memory_space=pltpu.SMEM),
            pl.BlockSpec((tt, TOP_K), row),
            pl.BlockSpec((tt, D), row),
            pl.BlockSpec((1, D), lambda i: (0, 0)),
            pl.BlockSpec(memory_space=pl.ANY),
        ],
        out_specs=pl.BlockSpec((tt, D), row),
        out_shape=jax.ShapeDtypeStruct((T, D), F32),
        scratch_shapes=[pltpu.VMEM((tt, D), F32), pltpu.VMEM((tt, D), F32),
                        pltpu.SemaphoreType.DMA(())],
        compiler_params=_params(_vmem_limit(blocks, scratch, temps), 1),
        name="combine",
    )(pos, gate_t, h1, g_post, o_sorted)


def _moe_ffn(ypk, logits_t, h1, g_post, w_gate, w_up, w_down, tri):
    T = h1.shape[0]
    n_tiles_max = (TOP_K * T) // EXPERT_ROW_TILE + N_EXPERTS
    assert n_tiles_max < TILE_INFO_LEN
    pos, gate, info = _route(logits_t, tri)
    xs = _dispatch(pos, ypk, n_tiles_max * EXPERT_ROW_TILE)
    o_sorted = _expert_ffn(info.reshape(TILE_INFO_LEN), xs, w_gate, w_up, w_down, n_chunks=2)
    return _combine(pos, gate.T, h1, g_post, o_sorted)


def _rope_inv_freq_lanes():
    lane = np.arange(V7X_LANES) % HEAD_DIM
    half = ROPE_DIM // 2
    idx = np.where(lane < half, lane, lane - half)
    inv = np.power(np.float32(ROPE_THETA), -(2.0 * idx.astype(np.float32)) / np.float32(ROPE_DIM))
    return jnp.asarray(np.where(lane < ROPE_DIM, inv, 0.0).astype(np.float32)).reshape(1, V7X_LANES)


def _group_tables(sgu_w):
    grp = np.arange(sgu_w) // HEAD_DIM
    gsum = (grp[:, None] == np.arange(V7X_LANES)[None, :]).astype(np.float32)
    return jnp.asarray(gsum, BF16), jnp.asarray(gsum.T, BF16)


def kernel(x, positions, pre_mix_norm, w_in, attn_sinks, sgu_ln_g, sgu_ln_b, sgu_w, sgu_b, attn_out_norm, sgu_out_norm, w_out, post_mix_norm, pre_ffn_norm, post_ffn_norm, ffn_w_gate, ffn_w_up, ffn_w_down, moe_router, moe_w_gate, moe_w_up, moe_w_down):
    B, S, D = x.shape
    T = B * S
    depth = pre_mix_norm.shape[0]
    n_q_heads = attn_sinks.shape[1]
    attn_w = n_q_heads * HEAD_DIM
    kv_w = (n_q_heads // Q_PER_KV) * HEAD_DIM
    sgu_width = sgu_ln_g.shape[1]
    assert kv_w == V7X_LANES and attn_w % V7X_LANES == 0 and sgu_width % V7X_LANES == 0
    assert S % TOKEN_TILE == 0 and T % DISPATCH_TILE == 0 and sgu_w.shape[2] == BLOCK
    assert moe_router.shape[-1] == N_EXPERTS

    invf = _rope_inv_freq_lanes()
    gsum, gbc = _group_tables(sgu_width)
    tri = jnp.asarray(np.triu(np.ones((ROUTE_CHUNK, ROUTE_CHUNK), np.float32)), BF16)
    row2 = lambda a: a.reshape(1, -1)

    h = x.reshape(T, D)
    pos = positions.reshape(T, 1)
    for layer in range(depth):
        q, kv, u, vn = _in_proj(
            h, pos, row2(pre_mix_norm[layer]), w_in[layer].astype(BF16), invf, gsum, gbc,
            row2(sgu_ln_g[layer]), row2(sgu_ln_b[layer]), attn_w=attn_w, kv_w=kv_w, sgu_w=sgu_width)
        is_moe = layer % 2 == 1
        i = layer // 2
        router = None
        if is_moe:
            wr = moe_router[i].T
            wr_hi = wr.astype(BF16)
            router = (wr_hi, (wr - wr_hi.astype(F32)).astype(BF16))
        outs = _mix(
            attn_sinks[layer], q, kv, u, vn, h, sgu_w[layer].astype(BF16), sgu_b[layer].T,
            row2(attn_out_norm[layer]), row2(sgu_out_norm[layer]), w_out[layer].astype(BF16),
            row2(post_mix_norm[layer]), row2(pre_ffn_norm[layer]), router, batch=B, seq=S)
        g_post = row2(post_ffn_norm[layer])
        if is_moe:
            h1, ypk, logits_t = outs
            h = _moe_ffn(ypk, logits_t, h1, g_post, moe_w_gate[i].astype(BF16),
                         moe_w_up[i].astype(BF16), moe_w_down[i].astype(BF16), tri)
        else:
            h1, y = outs
            h = _dense_ffn(y, h1, ffn_w_gate[i].astype(BF16), ffn_w_up[i].astype(BF16),
                           ffn_w_down[i].astype(BF16), g_post)
    return h.reshape(B, S, D)
```

```python
import functools

import numpy as np
import jax
import jax.numpy as jnp
from jax import lax
from jax.experimental import pallas as pl
from jax.experimental.pallas import tpu as pltpu

F32 = jnp.float32
BF16 = jnp.bfloat16
I32 = jnp.int32
U32 = jnp.uint32

V7X_LANES = 128
V7X_SUBLANES = 8
V7X_VMEM_BUDGET_BYTES = 56 * 1024 * 1024

HEAD_DIM = 64
Q_PER_KV = 4
WINDOW = 128
BLOCK = 128
ROPE_DIM = HEAD_DIM // 4
ROPE_THETA = 500000.0
N_EXPERTS = 8
TOP_K = 2
EPS = 1e-6
NEG_BIG = -1e30

TOKEN_TILE = 512
BLOCKS_PER_TILE = TOKEN_TILE // BLOCK
EXPERT_ROW_TILE = 512
ROUTE_CHUNK = 512
DISPATCH_TILE = 1024
COMBINE_TILE = 256
TILE_INFO_LEN = 256
PK_SUB = 4
OUT_SUB = 8


def _vmem_limit(block_bytes, scratch_bytes=0, temp_bytes=0):
    need = 2 * int(block_bytes) + int(scratch_bytes) + int(temp_bytes)
    return int(min(max(need, 16 * 1024 * 1024), V7X_VMEM_BUDGET_BYTES))


def _nbytes(shape, dtype):
    return int(np.prod(shape)) * jnp.dtype(dtype).itemsize


def _params(vmem_bytes, n_grid_dims):
    return pltpu.CompilerParams(
        dimension_semantics=("arbitrary",) * n_grid_dims,
        vmem_limit_bytes=vmem_bytes,
    )


def _rmsnorm(x, g):
    return x * lax.rsqrt(jnp.mean(x * x, axis=-1, keepdims=True) + EPS) * g


def _gelu_tanh(x):
    c = np.float32(np.sqrt(2.0 / np.pi))
    return x * (0.5 * (1.0 + jnp.tanh(c * (x + 0.044715 * (x * x * x)))))


def _silu(x):
    return x * (1.0 / (1.0 + jnp.exp(-x)))


def _split_bf16(x):
    hi = x.astype(BF16)
    lo = (x - hi.astype(F32)).astype(BF16)
    return hi, lo


def _dot(a, b):
    return jnp.dot(a, b, preferred_element_type=F32)


def _dot_nt(a, b):
    return lax.dot_general(a, b, (((1,), (1,)), ((), ())), preferred_element_type=F32)


def _rope_table_kernel(pos_ref, invf_ref, cos_ref, sin_ref):
    ang = pos_ref[...].astype(F32) * invf_ref[...]
    cos_ref[...] = jnp.cos(ang)
    sin_ref[...] = jnp.sin(ang)


def _rope_tables(pos, invf):
    T = pos.shape[0]
    tm = TOKEN_TILE
    row = lambda i: (i, 0)
    blocks = 3 * _nbytes((tm, V7X_LANES), F32)
    return pl.pallas_call(
        _rope_table_kernel,
        grid=(T // tm,),
        in_specs=[pl.BlockSpec((tm, 1), row), pl.BlockSpec((1, V7X_LANES), lambda i: (0, 0))],
        out_specs=[pl.BlockSpec((tm, V7X_LANES), row), pl.BlockSpec((tm, V7X_LANES), row)],
        out_shape=[jax.ShapeDtypeStruct((T, V7X_LANES), F32), jax.ShapeDtypeStruct((T, V7X_LANES), F32)],
        compiler_params=_params(_vmem_limit(blocks, 0, 8 * blocks), 1),
        name="rope_tables",
    )(pos, invf)


def _in_proj_kernel(h_ref, cos_ref, sin_ref, g_ref, w_ref, gsum_ref, gbc_ref, lng_ref, lnb_ref,
                    q_ref, kv_ref, u_ref, vn_ref, *, attn_w, kv_w, sgu_w):
    hn = _rmsnorm(h_ref[...], g_ref[...]).astype(BF16)
    z = _dot(hn, w_ref[...])

    lane = lax.broadcasted_iota(I32, (1, V7X_LANES), 1)
    hl = lane % HEAD_DIM
    cos = cos_ref[...]
    sin = sin_ref[...]
    half = ROPE_DIM // 2
    sin_a = jnp.where(hl < half, -sin, 0.0)
    sin_b = jnp.where((hl >= half) & (hl < ROPE_DIM), sin, 0.0)

    def rope(t):
        return (t * cos + pltpu.roll(t, V7X_LANES - half, 1) * sin_a
                + pltpu.roll(t, half, 1) * sin_b)

    scale = np.float32(HEAD_DIM ** -0.5)
    for c in range(attn_w // V7X_LANES):
        sl = slice(c * V7X_LANES, (c + 1) * V7X_LANES)
        q_ref[:, sl] = (rope(z[:, sl]) * scale).astype(BF16)

    low = lane < HEAD_DIM
    k = rope(z[:, attn_w:attn_w + kv_w])
    v = z[:, attn_w + kv_w:attn_w + 2 * kv_w]
    kr = pltpu.roll(k, HEAD_DIM, 1)
    vr = pltpu.roll(v, HEAD_DIM, 1)
    kv_ref[:, 0 * V7X_LANES:1 * V7X_LANES] = jnp.where(low, k, kr).astype(BF16)
    kv_ref[:, 1 * V7X_LANES:2 * V7X_LANES] = jnp.where(low, kr, k).astype(BF16)
    kv_ref[:, 2 * V7X_LANES:3 * V7X_LANES] = jnp.where(low, v, vr).astype(BF16)
    kv_ref[:, 3 * V7X_LANES:4 * V7X_LANES] = jnp.where(low, vr, v).astype(BF16)

    o = attn_w + 2 * kv_w
    u_ref[...] = _gelu_tanh(z[:, o:o + sgu_w])

    zv = _gelu_tanh(z[:, o + sgu_w:o + 2 * sgu_w])
    gsum = gsum_ref[...]
    gbc = gbc_ref[...]

    def group_mean(a):
        hi, lo = _split_bf16(a)
        s = (_dot(hi, gsum) + _dot(lo, gsum)) * np.float32(1.0 / HEAD_DIM)
        hi, lo = _split_bf16(s)
        return _dot(hi, gbc) + _dot(lo, gbc)

    d = zv - group_mean(zv)
    var = group_mean(d * d)
    vn_ref[...] = (d * lax.rsqrt(var + EPS) * lng_ref[...] + lnb_ref[...]).astype(BF16)


def _in_proj(h, cos, sin, g_pre, w_in, gsum, gbc, ln_g, ln_b, *, attn_w, kv_w, sgu_w):
    T, D = h.shape
    tm = TOKEN_TILE
    in_w = w_in.shape[1]
    row = lambda i: (i, 0)
    const = lambda i: (0, 0)
    blocks = (_nbytes((tm, D), F32) + 2 * _nbytes((tm, V7X_LANES), F32) + _nbytes((D, in_w), BF16)
              + _nbytes((tm, attn_w), BF16) + _nbytes((tm, 4 * V7X_LANES), BF16)
              + _nbytes((tm, sgu_w), F32) + _nbytes((tm, sgu_w), BF16))
    temps = 6 * _nbytes((tm, in_w), F32)
    kern = functools.partial(_in_proj_kernel, attn_w=attn_w, kv_w=kv_w, sgu_w=sgu_w)
    return pl.pallas_call(
        kern,
        grid=(T // tm,),
        in_specs=[
            pl.BlockSpec((tm, D), row),
            pl.BlockSpec((tm, V7X_LANES), row),
            pl.BlockSpec((tm, V7X_LANES), row),
            pl.BlockSpec((1, D), const),
            pl.BlockSpec((D, in_w), const),
            pl.BlockSpec((sgu_w, V7X_LANES), const),
            pl.BlockSpec((V7X_LANES, sgu_w), const),
            pl.BlockSpec((1, sgu_w), const),
            pl.BlockSpec((1, sgu_w), const),
        ],
        out_specs=[
            pl.BlockSpec((tm, attn_w), row),
            pl.BlockSpec((tm, 4 * V7X_LANES), row),
            pl.BlockSpec((tm, sgu_w), row),
            pl.BlockSpec((tm, sgu_w), row),
        ],
        out_shape=[
            jax.ShapeDtypeStruct((T, attn_w), BF16),
            jax.ShapeDtypeStruct((T, 4 * V7X_LANES), BF16),
            jax.ShapeDtypeStruct((T, sgu_w), F32),
            jax.ShapeDtypeStruct((T, sgu_w), BF16),
        ],
        compiler_params=_params(_vmem_limit(blocks, 0, temps), 1),
        name="in_proj",
    )(h, cos, sin, g_pre, w_in, gsum, gbc, ln_g, ln_b)


def _mix_kernel(sink_ref, q_ref, kv_ref, kvp_ref, u_ref, vn_ref, h_ref, ws_ref, bs_ref,
                ga_ref, gs_ref, wo_ref, gpost_ref, gffn_ref, *rest, moe, attn_w, sgu_w):
    if moe:
        wr_hi_ref, wr_lo_ref, h1_ref, ypk_ref, lg_ref, merged_ref = rest
    else:
        h1_ref, y_ref, merged_ref = rest

    first_tile = pl.program_id(1) == 0
    n_pairs = attn_w // V7X_LANES
    band = 2 * BLOCK

    n_kvh = n_pairs // 2
    row2 = lax.broadcasted_iota(I32, (2 * BLOCK, 2 * band), 0)
    row = row2 % BLOCK
    col = lax.broadcasted_iota(I32, (2 * BLOCK, 2 * band), 1) % band
    in_window = (col > row) & (col <= row + WINDOW)
    upper = lax.broadcasted_iota(I32, (2 * BLOCK, 1), 0) < BLOCK
    lane = lax.broadcasted_iota(I32, (1, V7X_LANES), 1)
    low = lane < HEAD_DIM
    tril = (lax.broadcasted_iota(I32, (2 * BLOCK, BLOCK), 0) % BLOCK
            >= lax.broadcasted_iota(I32, (2 * BLOCK, BLOCK), 1))
    zero_bf = jnp.zeros((), BF16)
    n_chunks = sgu_w // V7X_LANES
    w_cat = [jnp.where(tril, jnp.concatenate([ws_ref[2 * c], ws_ref[2 * c + 1]], axis=0), zero_bf)
             for c in range(n_chunks)]
    b_cat = [jnp.concatenate([bs_ref[:, 2 * c:2 * c + 1], bs_ref[:, 2 * c + 1:2 * c + 2]], axis=0)
             for c in range(n_chunks)]

    for blk in range(BLOCKS_PER_TILE):
        rows = slice(blk * BLOCK, (blk + 1) * BLOCK)
        if blk == 0:
            kv_prev = kvp_ref[...]
            mask = in_window & ((col >= BLOCK) | jnp.logical_not(first_tile))
        else:
            kv_prev = kv_ref[(blk - 1) * BLOCK:blk * BLOCK, :]
            mask = in_window
        kv_band = jnp.concatenate([kv_prev, kv_ref[rows, :]], axis=0)

        attn_parts = [None] * n_pairs
        for kvh in range(n_kvh):
            kk = kv_band[:, kvh * V7X_LANES:(kvh + 1) * V7X_LANES]
            vv = kv_band[:, (n_kvh + kvh) * V7X_LANES:(n_kvh + 1 + kvh) * V7X_LANES]
            k2 = jnp.concatenate([jnp.where(low, kk, zero_bf), jnp.where(low, zero_bf, kk)], axis=0)
            v2 = jnp.concatenate([jnp.where(low, vv, zero_bf), jnp.where(low, zero_bf, vv)], axis=0)
            p0, p1 = 2 * kvh, 2 * kvh + 1
            qq = jnp.concatenate([q_ref[rows, p0 * V7X_LANES:(p0 + 1) * V7X_LANES],
                                  q_ref[rows, p1 * V7X_LANES:(p1 + 1) * V7X_LANES]], axis=0)
            s = jnp.where(mask, _dot_nt(qq, k2), NEG_BIG)
            probs, inv = [], []
            for hh in range(2):
                sink = jnp.where(upper, sink_ref[2 * p0 + hh], sink_ref[2 * p1 + hh])
                sh = s[:, hh * band:(hh + 1) * band]
                m = jnp.maximum(jnp.max(sh, axis=-1, keepdims=True), sink)
                p = jnp.exp(sh - m)
                denom = jnp.sum(p, axis=-1, keepdims=True) + jnp.exp(sink - m)
                probs.append(p.astype(BF16))
                inv.append(1.0 / denom)
            o = _dot(jnp.concatenate(probs, axis=1), v2) * jnp.where(low, inv[0], inv[1])
            attn_parts[p0] = o[:BLOCK]
            attn_parts[p1] = o[BLOCK:]
        attn = jnp.concatenate(attn_parts, axis=1)

        sgu_parts = []
        for c in range(n_chunks):
            vn_c = vn_ref[rows, c * V7X_LANES:(c + 1) * V7X_LANES]
            mixed = _dot(w_cat[c], vn_c) + b_cat[c]
            sgu_parts.append(jnp.where(low, mixed[:BLOCK], mixed[BLOCK:]))
        sgu = u_ref[rows, :] * jnp.concatenate(sgu_parts, axis=1)

        merged_ref[rows, 0:attn_w] = _rmsnorm(attn, ga_ref[...]).astype(BF16)
        merged_ref[rows, attn_w:attn_w + sgu_w] = _rmsnorm(sgu, gs_ref[...]).astype(BF16)

    proj = _dot(merged_ref[...], wo_ref[...])
    h1 = h_ref[...] + _rmsnorm(proj, gpost_ref[...])
    h1_ref[...] = h1
    y = _rmsnorm(h1, gffn_ref[...])
    if not moe:
        y_ref[...] = y.astype(BF16)
    else:
        half = y.shape[1] // 2
        left = lax.bitcast_convert_type(y[:, :half].astype(BF16).astype(F32), U32)
        right = lax.bitcast_convert_type(y[:, half:].astype(BF16).astype(F32), U32)
        packed = left | (right >> 16)
        for j in range(PK_SUB):
            ypk_ref[pl.ds(j, y.shape[0], stride=PK_SUB), :] = packed[:, j * V7X_LANES:(j + 1) * V7X_LANES]
        y_hi, y_lo = _split_bf16(y)
        wr_hi = wr_hi_ref[...]
        lg_ref[...] = _dot_nt(wr_hi, y_hi) + _dot_nt(wr_hi, y_lo) + _dot_nt(wr_lo_ref[...], y_hi)


def _mix(sinks, q, kv, u, vn, h, w_s, b_s_t, g_attn, g_sgu, w_out, g_post, g_ffn, router, *,
         batch, seq):
    T, D = h.shape
    tq = TOKEN_TILE
    attn_w = q.shape[1]
    sgu_w = u.shape[1]
    tiles_per_seq = seq // tq
    blocks_per_seq = seq // BLOCK
    moe = router is not None
    n_groups = w_s.shape[0]

    row = lambda b, j: (b * tiles_per_seq + j, 0)
    prev = lambda b, j: (b * blocks_per_seq + jnp.maximum(j * BLOCKS_PER_TILE - 1, 0), 0)
    const2 = lambda b, j: (0, 0)
    const3 = lambda b, j: (0, 0, 0)

    in_specs = [
        pl.BlockSpec(memory_space=pltpu.SMEM),
        pl.BlockSpec((tq, attn_w), row),
        pl.BlockSpec((tq, 4 * V7X_LANES), row),
        pl.BlockSpec((BLOCK, 4 * V7X_LANES), prev),
        pl.BlockSpec((tq, sgu_w), row),
        pl.BlockSpec((tq, sgu_w), row),
        pl.BlockSpec((tq, D), row),
        pl.BlockSpec((n_groups, BLOCK, BLOCK), const3),
        pl.BlockSpec((BLOCK, n_groups), const2),
        pl.BlockSpec((1, attn_w), const2),
        pl.BlockSpec((1, sgu_w), const2),
        pl.BlockSpec((D, D), const2),
        pl.BlockSpec((1, D), const2),
        pl.BlockSpec((1, D), const2),
    ]
    args = [sinks, q, kv, kv, u, vn, h, w_s, b_s_t, g_attn, g_sgu, w_out, g_post, g_ffn]
    out_specs = [pl.BlockSpec((tq, D), row)]
    out_shape = [jax.ShapeDtypeStruct((T, D), F32)]
    blocks = (_nbytes((tq, attn_w), BF16) + 2 * _nbytes((tq, 4 * V7X_LANES), BF16)
              + _nbytes((tq, sgu_w), F32) + _nbytes((tq, sgu_w), BF16) + 2 * _nbytes((tq, D), F32)
              + _nbytes((n_groups, BLOCK, BLOCK), BF16) + _nbytes((D, D), BF16)
              + _nbytes((tq, D), F32))
    if moe:
        n_e = router[0].shape[0]
        in_specs += [pl.BlockSpec((n_e, D), const2), pl.BlockSpec((n_e, D), const2)]
        args += list(router)
        out_specs += [pl.BlockSpec((tq * PK_SUB, V7X_LANES), row),
                      pl.BlockSpec((n_e, tq), lambda b, j: (0, b * tiles_per_seq + j))]
        out_shape += [jax.ShapeDtypeStruct((T * PK_SUB, V7X_LANES), U32),
                      jax.ShapeDtypeStruct((n_e, T), F32)]
    else:
        out_specs += [pl.BlockSpec((tq, D), row)]
        out_shape += [jax.ShapeDtypeStruct((T, D), BF16)]
    scratch = _nbytes((tq, D), BF16)
    temps = 8 * _nbytes((tq, D), F32)
    kern = functools.partial(_mix_kernel, moe=moe, attn_w=attn_w, sgu_w=sgu_w)
    return pl.pallas_call(
        kern,
        grid=(batch, tiles_per_seq),
        in_specs=in_specs,
        out_specs=out_specs,
        out_shape=out_shape,
        scratch_shapes=[pltpu.VMEM((tq, D), BF16)],
        compiler_params=_params(_vmem_limit(blocks, scratch, temps), 2),
        name="mix_moe" if moe else "mix_dense",
    )(*args)


def _dense_ffn_kernel(y_ref, h_ref, wg_ref, wu_ref, wd_ref, g_ref, o_ref, *, chunk):
    y = y_ref[...]
    d_ff = wg_ref.shape[1]
    acc = jnp.zeros(o_ref.shape, F32)
    for c in range(d_ff // chunk):
        sl = slice(c * chunk, (c + 1) * chunk)
        a = _silu(_dot(y, wg_ref[:, sl])) * _dot(y, wu_ref[:, sl])
        acc = acc + _dot(a.astype(BF16), wd_ref[sl, :])
    o_ref[...] = h_ref[...] + _rmsnorm(acc, g_ref[...])


def _dense_ffn(y, h1, w_gate, w_up, w_down, g_post):
    T, D = h1.shape
    d_ff = w_gate.shape[1]
    tm = TOKEN_TILE
    chunk = 2 * V7X_LANES
    assert d_ff % chunk == 0
    row = lambda i: (i, 0)
    const = lambda i: (0, 0)
    blocks = (_nbytes((tm, D), BF16) + 2 * _nbytes((tm, D), F32) + 3 * _nbytes((D, d_ff), BF16))
    temps = 4 * _nbytes((tm, D), F32)
    return pl.pallas_call(
        functools.partial(_dense_ffn_kernel, chunk=chunk),
        grid=(T // tm,),
        in_specs=[
            pl.BlockSpec((tm, D), row),
            pl.BlockSpec((tm, D), row),
            pl.BlockSpec((D, d_ff), const),
            pl.BlockSpec((D, d_ff), const),
            pl.BlockSpec((d_ff, D), const),
            pl.BlockSpec((1, D), const),
        ],
        out_specs=pl.BlockSpec((tm, D), row),
        out_shape=jax.ShapeDtypeStruct((T, D), F32),
        compiler_params=_params(_vmem_limit(blocks, 0, temps), 1),
        name="dense_ffn",
    )(y, h1, w_gate, w_up, w_down, g_post)


def _route_kernel(lg_ref, tri_ref, pos_ref, gate_ref, info_ref, oh_ref, rank_ref):
    n_e, T = lg_ref.shape
    lg = lg_ref[...]
    eidx = lax.broadcasted_iota(I32, (n_e, T), 0)
    m1 = jnp.max(lg, axis=0, keepdims=True)
    i1 = jnp.min(jnp.where(lg == m1, eidx, n_e), axis=0, keepdims=True)
    oh1 = eidx == i1
    lg2 = jnp.where(oh1, -jnp.inf, lg)
    m2 = jnp.max(lg2, axis=0, keepdims=True)
    i2 = jnp.min(jnp.where(lg2 == m2, eidx, n_e), axis=0, keepdims=True)
    oh2 = eidx == i2
    e2 = jnp.exp(m2 - m1)
    gate_ref[0:1, :] = 1.0 / (1.0 + e2)
    gate_ref[1:2, :] = e2 / (1.0 + e2)

    oh_ref[...] = jnp.where(oh1 | oh2, 1.0, 0.0)

    tri = tri_ref[...]

    def body(i, carry):
        start = pl.multiple_of(i * ROUTE_CHUNK, ROUTE_CHUNK)
        ohc = oh_ref[:, pl.ds(start, ROUTE_CHUNK)]
        inc = _dot(ohc.astype(BF16), tri)
        rank_ref[:, pl.ds(start, ROUTE_CHUNK)] = carry + inc - ohc
        return carry + inc[:, ROUTE_CHUNK - 1:ROUTE_CHUNK]

    count = lax.fori_loop(0, T // ROUTE_CHUNK, body, jnp.zeros((n_e, 1), F32))

    tm = np.float32(EXPERT_ROW_TILE)
    n_tiles = jnp.floor((count + (tm - 1.0)) * (1.0 / tm))
    e_col = lax.broadcasted_iota(I32, (n_e, 1), 0)
    first_tile = jnp.zeros((n_e, 1), F32)
    for e in range(n_e - 1):
        nt_e = jnp.sum(jnp.where(e_col == e, n_tiles, 0.0), axis=0, keepdims=True)
        first_tile = first_tile + jnp.where(e_col > e, nt_e, 0.0)
    row0 = first_tile * tm + rank_ref[...]
    pos_ref[0:1, :] = jnp.sum(jnp.where(oh1, row0, 0.0), axis=0, keepdims=True).astype(I32)
    pos_ref[1:2, :] = jnp.sum(jnp.where(oh2, row0, 0.0), axis=0, keepdims=True).astype(I32)

    end_tile = (first_tile + n_tiles).astype(I32)
    tile = lax.broadcasted_iota(I32, (n_e, TILE_INFO_LEN), 1)
    owner = jnp.sum(jnp.where(end_tile <= tile, 1, 0), axis=0, keepdims=True)
    total = jnp.sum(n_tiles, axis=0, keepdims=True).astype(I32)
    lane = lax.broadcasted_iota(I32, (1, TILE_INFO_LEN), 1)
    info_ref[...] = jnp.where(lane == TILE_INFO_LEN - 1, total, jnp.minimum(owner, n_e - 1))


def _route(logits_t, tri):
    n_e, T = logits_t.shape
    blocks = (_nbytes((n_e, T), F32) + _nbytes((ROUTE_CHUNK, ROUTE_CHUNK), BF16)
              + _nbytes((8, T), I32) + _nbytes((8, T), F32))
    scratch = 2 * _nbytes((n_e, T), F32)
    temps = 12 * _nbytes((n_e, T), F32)
    return pl.pallas_call(
        _route_kernel,
        out_shape=[
            jax.ShapeDtypeStruct((TOP_K, T), I32),
            jax.ShapeDtypeStruct((TOP_K, T), F32),
            jax.ShapeDtypeStruct((1, TILE_INFO_LEN), I32),
        ],
        scratch_shapes=[pltpu.VMEM((n_e, T), F32), pltpu.VMEM((n_e, T), F32)],
        compiler_params=pltpu.CompilerParams(vmem_limit_bytes=_vmem_limit(blocks, scratch, temps)),
        name="route",
    )(logits_t, tri)


def _slab_copy(src, src_row, dst, dst_row, sub, sem):
    s0 = pl.multiple_of(src_row * sub, sub)
    d0 = pl.multiple_of(dst_row * sub, sub)
    return pltpu.make_async_copy(src.at[pl.ds(s0, sub)], dst.at[pl.ds(d0, sub)], sem)


def _dispatch_kernel(pos0_ref, pos1_ref, ypk_ref, xs_in_hbm, xs_hbm, sem):
    del xs_in_hbm
    tt = pos0_ref.shape[0]
    pos_refs = (pos0_ref, pos1_ref)

    def issue(t, carry):
        for k in range(TOP_K):
            _slab_copy(ypk_ref, t, xs_hbm, pos_refs[k][t], PK_SUB, sem).start(priority=k)
        return carry

    def drain(t, carry):
        for k in range(TOP_K):
            _slab_copy(ypk_ref, 0, xs_hbm, 0, PK_SUB, sem).wait()
        return carry

    lax.fori_loop(0, tt, issue, 0, unroll=8)
    lax.fori_loop(0, tt, drain, 0, unroll=8)


def _dispatch(pos, ypk, n_rows):
    T = ypk.shape[0] // PK_SUB
    W = V7X_LANES
    tt = DISPATCH_TILE
    xs0 = jnp.zeros((n_rows * PK_SUB, W), U32)
    return pl.pallas_call(
        _dispatch_kernel,
        grid=(T // tt,),
        in_specs=[
            pl.BlockSpec((tt,), lambda i: (i,), memory_space=pltpu.SMEM),
            pl.BlockSpec((tt,), lambda i: (i,), memory_space=pltpu.SMEM),
            pl.BlockSpec((tt * PK_SUB, W), lambda i: (i, 0)),
            pl.BlockSpec(memory_space=pl.ANY),
        ],
        out_specs=pl.BlockSpec(memory_space=pl.ANY),
        out_shape=jax.ShapeDtypeStruct((n_rows * PK_SUB, W), U32),
        scratch_shapes=[pltpu.SemaphoreType.DMA(())],
        input_output_aliases={3: 0},
        compiler_params=pltpu.CompilerParams(
            dimension_semantics=("arbitrary",), has_side_effects=True,
            vmem_limit_bytes=_vmem_limit(_nbytes((tt * PK_SUB, W), U32))),
        name="dispatch",
    )(pos[0], pos[1], ypk, xs0)


def _expert_kernel(info_ref, xs_ref, wg_ref, wu_ref, wd_ref, o_ref, acc_ref):
    i = pl.program_id(0)
    c = pl.program_id(1)
    valid = i < info_ref[TILE_INFO_LEN - 1]
    tm = acc_ref.shape[0]

    @pl.when(c == 0)
    def _():
        acc_ref[...] = jnp.zeros(acc_ref.shape, F32)

    @pl.when(valid)
    def _():
        words = [xs_ref[pl.ds(j, tm, stride=PK_SUB), :] for j in range(PK_SUB)]
        left = [lax.bitcast_convert_type(w & np.uint32(0xFFFF0000), F32).astype(BF16) for w in words]
        right = [lax.bitcast_convert_type(w << 16, F32).astype(BF16) for w in words]
        x = jnp.concatenate(left + right, axis=1)
        a = _silu(_dot(x, wg_ref[0])) * _dot(x, wu_ref[0])
        acc_ref[...] += _dot(a.astype(BF16), wd_ref[0])

    @pl.when(c == pl.num_programs(1) - 1)
    def _():
        for j in range(OUT_SUB):
            o_ref[pl.ds(j, tm, stride=OUT_SUB), :] = acc_ref[:, j * V7X_LANES:(j + 1) * V7X_LANES]


def _expert_ffn(tile_info, xs, w_gate, w_up, w_down, n_chunks):
    R = xs.shape[0] // PK_SUB
    W = V7X_LANES
    n_e, D, d_ff = w_gate.shape
    assert D == OUT_SUB * V7X_LANES
    tm = EXPERT_ROW_TILE
    ck = d_ff // n_chunks
    last = TILE_INFO_LEN - 1

    def tile_of(i, info):
        return jnp.minimum(i, info[last] - 1)

    def chunk_of(i, c, info):
        return jnp.where(i < info[last], c, n_chunks - 1)

    grid_spec = pltpu.PrefetchScalarGridSpec(
        num_scalar_prefetch=1,
        grid=(R // tm, n_chunks),
        in_specs=[
            pl.BlockSpec((tm * PK_SUB, W), lambda i, c, info: (tile_of(i, info), 0)),
            pl.BlockSpec((1, D, ck), lambda i, c, info: (info[tile_of(i, info)], 0, chunk_of(i, c, info))),
            pl.BlockSpec((1, D, ck), lambda i, c, info: (info[tile_of(i, info)], 0, chunk_of(i, c, info))),
            pl.BlockSpec((1, ck, D), lambda i, c, info: (info[tile_of(i, info)], chunk_of(i, c, info), 0)),
        ],
        out_specs=pl.BlockSpec((tm * OUT_SUB, V7X_LANES), lambda i, c, info: (i, 0)),
        scratch_shapes=[pltpu.VMEM((tm, D), F32)],
    )
    blocks = (_nbytes((tm * PK_SUB, W), U32) + 3 * _nbytes((D, ck), BF16) + _nbytes((tm, D), F32))
    temps = 3 * _nbytes((tm, ck), F32) + 2 * _nbytes((tm, D), F32)
    return pl.pallas_call(
        _expert_kernel,
        grid_spec=grid_spec,
        out_shape=jax.ShapeDtypeStruct((R * OUT_SUB, V7X_LANES), F32),
        compiler_params=_params(_vmem_limit(blocks, _nbytes((tm, D), F32), temps), 2),
        name="expert_ffn",
    )(tile_info, xs, w_gate, w_up, w_down)


def _combine_kernel(pos0_ref, pos1_ref, gate_ref, h_ref, g_ref, os_hbm, o_ref, a_ref, b_ref, sem):
    tt = h_ref.shape[0]
    bufs = (a_ref, b_ref)
    pos_refs = (pos0_ref, pos1_ref)

    def issue(t, carry):
        for k in range(TOP_K):
            _slab_copy(os_hbm, pos_refs[k][t], bufs[k], t, OUT_SUB, sem).start(priority=k)
        return carry

    def drain(t, carry):
        for k in range(TOP_K):
            _slab_copy(os_hbm, 0, bufs[k], 0, OUT_SUB, sem).wait()
        return carry

    lax.fori_loop(0, tt, issue, 0, unroll=8)
    lax.fori_loop(0, tt, drain, 0, unroll=8)
    rows = lambda ref: jnp.concatenate(
        [ref[pl.ds(j, tt, stride=OUT_SUB), :] for j in range(OUT_SUB)], axis=1)
    f = gate_ref[:, 0:1] * rows(a_ref) + gate_ref[:, 1:2] * rows(b_ref)
    o_ref[...] = h_ref[...] + _rmsnorm(f, g_ref[...])


def _combine(pos, gate_t, h1, g_post, o_sorted):
    T, D = h1.shape
    tt = COMBINE_TILE
    row = lambda i: (i, 0)
    blocks = (_nbytes((tt, V7X_LANES), F32) + 2 * _nbytes((tt, D), F32))
    scratch = 2 * _nbytes((tt, D), F32)
    temps = 3 * _nbytes((tt, D), F32)
    return pl.pallas_call(
        _combine_kernel,
        grid=(T // tt,),
        in_specs=[
            pl.BlockSpec((tt,), lambda i: (i,), memory_space=pltpu.SMEM),
            pl.BlockSpec((tt,), lambda i: (i,), memory_space=pltpu.SMEM),
            pl.BlockSpec((tt, TOP_K), row),
            pl.BlockSpec((tt, D), row),
            pl.BlockSpec((1, D), lambda i: (0, 0)),
            pl.BlockSpec(memory_space=pl.ANY),
        ],
        out_specs=pl.BlockSpec((tt, D), row),
        out_shape=jax.ShapeDtypeStruct((T, D), F32),
        scratch_shapes=[pltpu.VMEM((tt * OUT_SUB, V7X_LANES), F32),
                        pltpu.VMEM((tt * OUT_SUB, V7X_LANES), F32),
                        pltpu.SemaphoreType.DMA(())],
        compiler_params=_params(_vmem_limit(blocks, scratch, temps), 1),
        name="combine",
    )(pos[0], pos[1], gate_t, h1, g_post, o_sorted)


def _moe_ffn(ypk, logits_t, h1, g_post, w_gate, w_up, w_down, tri):
    T = h1.shape[0]
    n_tiles_max = (TOP_K * T) // EXPERT_ROW_TILE + N_EXPERTS
    assert n_tiles_max < TILE_INFO_LEN
    pos, gate, info = _route(logits_t, tri)
    xs = _dispatch(pos, ypk, n_tiles_max * EXPERT_ROW_TILE)
    o_sorted = _expert_ffn(info.reshape(TILE_INFO_LEN), xs, w_gate, w_up, w_down, n_chunks=2)
    return _combine(pos, gate.T, h1, g_post, o_sorted)


def _rope_inv_freq_lanes():
    lane = np.arange(V7X_LANES) % HEAD_DIM
    half = ROPE_DIM // 2
    idx = np.where(lane < half, lane, lane - half)
    inv = np.power(np.float32(ROPE_THETA), -(2.0 * idx.astype(np.float32)) / np.float32(ROPE_DIM))
    return jnp.asarray(np.where(lane < ROPE_DIM, inv, 0.0).astype(np.float32)).reshape(1, V7X_LANES)


def _group_tables(sgu_w):
    grp = np.arange(sgu_w) // HEAD_DIM
    gsum = (grp[:, None] == np.arange(V7X_LANES)[None, :]).astype(np.float32)
    return jnp.asarray(gsum, BF16), jnp.asarray(gsum.T, BF16)


def kernel(x, positions, pre_mix_norm, w_in, attn_sinks, sgu_ln_g, sgu_ln_b, sgu_w, sgu_b, attn_out_norm, sgu_out_norm, w_out, post_mix_norm, pre_ffn_norm, post_ffn_norm, ffn_w_gate, ffn_w_up, ffn_w_down, moe_router, moe_w_gate, moe_w_up, moe_w_down):
    B, S, D = x.shape
    T = B * S
    depth = pre_mix_norm.shape[0]
    n_q_heads = attn_sinks.shape[1]
    attn_w = n_q_heads * HEAD_DIM
    kv_w = (n_q_heads // Q_PER_KV) * HEAD_DIM
    sgu_width = sgu_ln_g.shape[1]
    assert kv_w == V7X_LANES and attn_w % V7X_LANES == 0 and sgu_width % V7X_LANES == 0
    assert S % TOKEN_TILE == 0 and T % DISPATCH_TILE == 0 and sgu_w.shape[2] == BLOCK
    assert moe_router.shape[-1] == N_EXPERTS

    invf = _rope_inv_freq_lanes()
    gsum, gbc = _group_tables(sgu_width)
    tri = jnp.asarray(np.triu(np.ones((ROUTE_CHUNK, ROUTE_CHUNK), np.float32)), BF16)
    row2 = lambda a: a.reshape(1, -1)

    h = x.reshape(T, D)
    cos, sin = _rope_tables(positions.reshape(T, 1), invf)
    for layer in range(depth):
        q, kv, u, vn = _in_proj(
            h, cos, sin, row2(pre_mix_norm[layer]), w_in[layer].astype(BF16), gsum, gbc,
            row2(sgu_ln_g[layer]), row2(sgu_ln_b[layer]), attn_w=attn_w, kv_w=kv_w, sgu_w=sgu_width)
        is_moe = layer % 2 == 1
        i = layer // 2
        router = None
        if is_moe:
            wr = moe_router[i].T
            wr_hi = wr.astype(BF16)
            router = (wr_hi, (wr - wr_hi.astype(F32)).astype(BF16))
        outs = _mix(
            attn_sinks[layer], q, kv, u, vn, h, sgu_w[layer].astype(BF16), sgu_b[layer].T,
            row2(attn_out_norm[layer]), row2(sgu_out_norm[layer]), w_out[layer].astype(BF16),
            row2(post_mix_norm[layer]), row2(pre_ffn_norm[layer]), router, batch=B, seq=S)
        g_post = row2(post_ffn_norm[layer])
        if is_moe:
            h1, ypk, logits_t = outs
            h = _moe_ffn(ypk, logits_t, h1, g_post, moe_w_gate[i].astype(BF16),
                         moe_w_up[i].astype(BF16), moe_w_down[i].astype(BF16), tri)
        else:
            h1, y = outs
            h = _dense_ffn(y, h1, ffn_w_gate[i].astype(BF16), ffn_w_up[i].astype(BF16),
                           ffn_w_down[i].astype(BF16), g_post)
    return h.reshape(B, S, D)
```

```python
import functools

import numpy as np
import jax
import jax.numpy as jnp
from jax import lax
from jax.experimental import pallas as pl
from jax.experimental.pallas import tpu as pltpu

F32 = jnp.float32
BF16 = jnp.bfloat16
I32 = jnp.int32
U32 = jnp.uint32

V7X_LANES = 128
V7X_SUBLANES = 8
V7X_VMEM_BUDGET_BYTES = 56 * 1024 * 1024

HEAD_DIM = 64
Q_PER_KV = 4
WINDOW = 128
BLOCK = 128
ROPE_DIM = HEAD_DIM // 4
ROPE_THETA = 500000.0
N_EXPERTS = 8
TOP_K = 2
EPS = 1e-6
NEG_BIG = -1e30

TOKEN_TILE = 512
BLOCKS_PER_TILE = TOKEN_TILE // BLOCK
IN_PROJ_SUBTILE = 256
MIX_OUT_SUBTILE = 256
EXPERT_ROW_TILE = 512
ROUTE_CHUNK = 512
DISPATCH_TILE = 1024
COMBINE_TILE = 512
TILE_INFO_LEN = 256
PK_SUB = 4
OUT_SUB = 8


def _vmem_limit(block_bytes, scratch_bytes=0, temp_bytes=0):
    need = 2 * int(block_bytes) + int(scratch_bytes) + int(temp_bytes)
    return int(min(max(need, 16 * 1024 * 1024), V7X_VMEM_BUDGET_BYTES))


def _nbytes(shape, dtype):
    return int(np.prod(shape)) * jnp.dtype(dtype).itemsize


def _params(vmem_bytes, n_grid_dims):
    return pltpu.CompilerParams(
        dimension_semantics=("arbitrary",) * n_grid_dims,
        vmem_limit_bytes=vmem_bytes,
    )


def _rmsnorm(x, g):
    return x * lax.rsqrt(jnp.mean(x * x, axis=-1, keepdims=True) + EPS) * g


def _gelu_tanh(x):
    c = np.float32(np.sqrt(2.0 / np.pi))
    return x * (0.5 * (1.0 + jnp.tanh(c * (x + 0.044715 * (x * x * x)))))


def _silu(x):
    return x * (1.0 / (1.0 + jnp.exp(-x)))


def _split_bf16(x):
    hi = x.astype(BF16)
    lo = (x - hi.astype(F32)).astype(BF16)
    return hi, lo


def _pack_bf16_pairs(x):
    half = x.shape[1] // 2
    left = lax.bitcast_convert_type(x[:, :half].astype(BF16).astype(F32), U32)
    right = lax.bitcast_convert_type(x[:, half:].astype(BF16).astype(F32), U32)
    return left | (right >> 16)


def _unpack_bf16_pairs(words):
    left = [lax.bitcast_convert_type(w & np.uint32(0xFFFF0000), F32) for w in words]
    right = [lax.bitcast_convert_type(w << 16, F32) for w in words]
    return left, right


def _dot(a, b):
    return jnp.dot(a, b, preferred_element_type=F32)


def _dot_nt(a, b):
    return lax.dot_general(a, b, (((1,), (1,)), ((), ())), preferred_element_type=F32)


def _rope_table_kernel(pos_ref, invf_ref, cos_ref, sin_ref):
    ang = pos_ref[...].astype(F32) * invf_ref[...]
    cos_ref[...] = jnp.cos(ang)
    sin_ref[...] = jnp.sin(ang)


def _rope_tables(pos, invf):
    T = pos.shape[0]
    tm = TOKEN_TILE
    row = lambda i: (i, 0)
    blocks = 3 * _nbytes((tm, V7X_LANES), F32)
    return pl.pallas_call(
        _rope_table_kernel,
        grid=(T // tm,),
        in_specs=[pl.BlockSpec((tm, 1), row), pl.BlockSpec((1, V7X_LANES), lambda i: (0, 0))],
        out_specs=[pl.BlockSpec((tm, V7X_LANES), row), pl.BlockSpec((tm, V7X_LANES), row)],
        out_shape=[jax.ShapeDtypeStruct((T, V7X_LANES), F32), jax.ShapeDtypeStruct((T, V7X_LANES), F32)],
        compiler_params=_params(_vmem_limit(blocks, 0, 8 * blocks), 1),
        name="rope_tables",
    )(pos, invf)


def _in_proj_kernel(h_ref, cos_ref, sin_ref, g_ref, w_ref, wvt_ref, lng_ref, lnb_ref,
                    q_ref, kv_ref, u_ref, vnt_ref, *, attn_w, kv_w, sgu_w):
    lane = lax.broadcasted_iota(I32, (1, V7X_LANES), 1)
    hl = lane % HEAD_DIM
    low = lane < HEAD_DIM
    half = ROPE_DIM // 2
    scale = np.float32(HEAD_DIM ** -0.5)
    tm = h_ref.shape[0]

    for r0 in range(0, tm, IN_PROJ_SUBTILE):
        rs = slice(r0, r0 + IN_PROJ_SUBTILE)
        hn = _rmsnorm(h_ref[rs, :], g_ref[...]).astype(BF16)
        z = _dot(hn, w_ref[...])

        cos = cos_ref[rs, :]
        sin = sin_ref[rs, :]
        sin_a = jnp.where(hl < half, -sin, 0.0)
        sin_b = jnp.where((hl >= half) & (hl < ROPE_DIM), sin, 0.0)

        def rope(t):
            return (t * cos + pltpu.roll(t, V7X_LANES - half, 1) * sin_a
                    + pltpu.roll(t, half, 1) * sin_b)

        for c in range(attn_w // V7X_LANES):
            sl = slice(c * V7X_LANES, (c + 1) * V7X_LANES)
            q_ref[rs, sl] = (rope(z[:, sl]) * scale).astype(BF16)

        k = rope(z[:, attn_w:attn_w + kv_w])
        v = z[:, attn_w + kv_w:attn_w + 2 * kv_w]
        kr = pltpu.roll(k, HEAD_DIM, 1)
        vr = pltpu.roll(v, HEAD_DIM, 1)
        kv_ref[rs, 0 * V7X_LANES:1 * V7X_LANES] = jnp.where(low, k, kr).astype(BF16)
        kv_ref[rs, 1 * V7X_LANES:2 * V7X_LANES] = jnp.where(low, kr, k).astype(BF16)
        kv_ref[rs, 2 * V7X_LANES:3 * V7X_LANES] = jnp.where(low, v, vr).astype(BF16)
        kv_ref[rs, 3 * V7X_LANES:4 * V7X_LANES] = jnp.where(low, vr, v).astype(BF16)

        o = attn_w + 2 * kv_w
        u_ref[rs, :] = _gelu_tanh(z[:, o:o + sgu_w])

        zvt = _gelu_tanh(_dot_nt(wvt_ref[...], hn))
        g3 = zvt.reshape(sgu_w // HEAD_DIM, HEAD_DIM, IN_PROJ_SUBTILE)
        d = g3 - jnp.mean(g3, axis=1, keepdims=True)
        var = jnp.mean(d * d, axis=1, keepdims=True)
        vn = (d * lax.rsqrt(var + EPS)).reshape(sgu_w, IN_PROJ_SUBTILE)
        for c in range(IN_PROJ_SUBTILE // V7X_LANES):
            sl = slice(c * V7X_LANES, (c + 1) * V7X_LANES)
            vnt_ref[:, r0 + c * V7X_LANES:r0 + (c + 1) * V7X_LANES] = (
                vn[:, sl] * lng_ref[...] + lnb_ref[...]).astype(BF16)


def _in_proj(h, cos, sin, g_pre, w_main, w_zv_t, ln_g, ln_b, *, attn_w, kv_w, sgu_w):
    T, D = h.shape
    tm = TOKEN_TILE
    in_w = w_main.shape[1] + sgu_w
    row = lambda i: (i, 0)
    const = lambda i: (0, 0)
    blocks = (_nbytes((tm, D), F32) + 2 * _nbytes((tm, V7X_LANES), F32) + _nbytes((D, in_w), BF16)
              + _nbytes((tm, attn_w), BF16) + _nbytes((tm, 4 * V7X_LANES), BF16)
              + _nbytes((tm, sgu_w), F32) + _nbytes((tm, sgu_w), BF16))
    temps = 6 * _nbytes((tm, in_w), F32)
    kern = functools.partial(_in_proj_kernel, attn_w=attn_w, kv_w=kv_w, sgu_w=sgu_w)
    return pl.pallas_call(
        kern,
        grid=(T // tm,),
        in_specs=[
            pl.BlockSpec((tm, D), row),
            pl.BlockSpec((tm, V7X_LANES), row),
            pl.BlockSpec((tm, V7X_LANES), row),
            pl.BlockSpec((1, D), const),
            pl.BlockSpec((D, in_w - sgu_w), const),
            pl.BlockSpec((sgu_w, D), const),
            pl.BlockSpec((sgu_w, V7X_LANES), const),
            pl.BlockSpec((sgu_w, V7X_LANES), const),
        ],
        out_specs=[
            pl.BlockSpec((tm, attn_w), row),
            pl.BlockSpec((tm, 4 * V7X_LANES), row),
            pl.BlockSpec((tm, sgu_w), row),
            pl.BlockSpec((sgu_w, tm), lambda i: (0, i)),
        ],
        out_shape=[
            jax.ShapeDtypeStruct((T, attn_w), BF16),
            jax.ShapeDtypeStruct((T, 4 * V7X_LANES), BF16),
            jax.ShapeDtypeStruct((T, sgu_w), F32),
            jax.ShapeDtypeStruct((sgu_w, T), BF16),
        ],
        compiler_params=_params(_vmem_limit(blocks, 0, temps), 1),
        name="in_proj",
    )(h, cos, sin, g_pre, w_main, w_zv_t, ln_g, ln_b)


def _mix_kernel(sink_ref, q_ref, kv_ref, kvp_ref, u_ref, vnt_ref, h_ref, wst_ref, bs_ref,
                ga_ref, gs_ref, wo_ref, gpost_ref, gffn_ref, *rest, moe, attn_w, sgu_w):
    if moe:
        wr_hi_ref, wr_lo_ref, h1_ref, ypk_ref, lg_ref, merged_ref = rest
    else:
        h1_ref, y_ref, merged_ref = rest

    first_tile = pl.program_id(1) == 0
    n_pairs = attn_w // V7X_LANES
    band = 2 * BLOCK

    n_kvh = n_pairs // 2
    row2 = lax.broadcasted_iota(I32, (2 * BLOCK, 2 * band), 0)
    row = row2 % BLOCK
    col = lax.broadcasted_iota(I32, (2 * BLOCK, 2 * band), 1) % band
    in_window = (col > row) & (col <= row + WINDOW)
    upper = lax.broadcasted_iota(I32, (2 * BLOCK, 1), 0) < BLOCK
    lane = lax.broadcasted_iota(I32, (1, V7X_LANES), 1)
    low = lane < HEAD_DIM
    zero_bf = jnp.zeros((), BF16)
    n_chunks = sgu_w // V7X_LANES

    causal_t = (lax.broadcasted_iota(I32, (BLOCK, BLOCK), 0)
                <= lax.broadcasted_iota(I32, (BLOCK, BLOCK), 1))
    top = lax.broadcasted_iota(I32, (2 * HEAD_DIM, 1), 0) < HEAD_DIM
    mixed_t = []
    for g in range(2 * n_chunks):
        lhs = jnp.concatenate(
            [vnt_ref[g * HEAD_DIM:(g + 1) * HEAD_DIM, b * BLOCK:(b + 1) * BLOCK]
             for b in range(BLOCKS_PER_TILE)], axis=0)
        mixed_t.append(_dot(lhs, jnp.where(causal_t, wst_ref[g], zero_bf)))
    bias_t = [jnp.where(top, bs_ref[2 * c:2 * c + 1, :], bs_ref[2 * c + 1:2 * c + 2, :])
              for c in range(n_chunks)]

    for blk in range(BLOCKS_PER_TILE):
        rows = slice(blk * BLOCK, (blk + 1) * BLOCK)
        if blk == 0:
            kv_prev = kvp_ref[...]
            mask = in_window & ((col >= BLOCK) | jnp.logical_not(first_tile))
        else:
            kv_prev = kv_ref[(blk - 1) * BLOCK:blk * BLOCK, :]
            mask = in_window
        kv_band = jnp.concatenate([kv_prev, kv_ref[rows, :]], axis=0)

        attn_parts = [None] * n_pairs
        for kvh in range(n_kvh):
            kk = kv_band[:, kvh * V7X_LANES:(kvh + 1) * V7X_LANES]
            vv = kv_band[:, (n_kvh + kvh) * V7X_LANES:(n_kvh + 1 + kvh) * V7X_LANES]
            k2 = jnp.concatenate([jnp.where(low, kk, zero_bf), jnp.where(low, zero_bf, kk)], axis=0)
            v2 = jnp.concatenate([jnp.where(low, vv, zero_bf), jnp.where(low, zero_bf, vv)], axis=0)
            p0, p1 = 2 * kvh, 2 * kvh + 1
            qq = jnp.concatenate([q_ref[rows, p0 * V7X_LANES:(p0 + 1) * V7X_LANES],
                                  q_ref[rows, p1 * V7X_LANES:(p1 + 1) * V7X_LANES]], axis=0)
            s = jnp.where(mask, _dot_nt(qq, k2), NEG_BIG)
            probs, inv = [], []
            for hh in range(2):
                sink = jnp.where(upper, sink_ref[2 * p0 + hh], sink_ref[2 * p1 + hh])
                sh = s[:, hh * band:(hh + 1) * band]
                m = jnp.maximum(jnp.max(sh, axis=-1, keepdims=True), sink)
                p = jnp.exp(sh - m)
                denom = jnp.sum(p, axis=-1, keepdims=True) + jnp.exp(sink - m)
                probs.append(p.astype(BF16))
                inv.append(1.0 / denom)
            o = _dot(jnp.concatenate(probs, axis=1), v2) * jnp.where(low, inv[0], inv[1])
            attn_parts[p0] = o[:BLOCK]
            attn_parts[p1] = o[BLOCK:]
        attn = jnp.concatenate(attn_parts, axis=1)

        sgu_parts = []
        for c in range(n_chunks):
            drows = slice(blk * HEAD_DIM, (blk + 1) * HEAD_DIM)
            pair_t = jnp.concatenate([mixed_t[2 * c][drows], mixed_t[2 * c + 1][drows]], axis=0)
            sgu_parts.append((pair_t + bias_t[c]).T)
        sgu = u_ref[rows, :] * jnp.concatenate(sgu_parts, axis=1)

        merged_ref[rows, 0:attn_w] = _rmsnorm(attn, ga_ref[...]).astype(BF16)
        merged_ref[rows, attn_w:attn_w + sgu_w] = _rmsnorm(sgu, gs_ref[...]).astype(BF16)

    ys = []
    for r0 in range(0, h_ref.shape[0], MIX_OUT_SUBTILE):
        rs = slice(r0, r0 + MIX_OUT_SUBTILE)
        proj = _dot(merged_ref[rs, :], wo_ref[...])
        h1 = h_ref[rs, :] + _rmsnorm(proj, gpost_ref[...])
        h1_ref[rs, :] = h1
        y = _rmsnorm(h1, gffn_ref[...])
        if not moe:
            y_ref[rs, :] = y.astype(BF16)
        else:
            ys.append(y)
    if moe:
        y = jnp.concatenate(ys, axis=0)
        packed = _pack_bf16_pairs(y)
        for j in range(PK_SUB):
            ypk_ref[pl.ds(j, y.shape[0], stride=PK_SUB), :] = packed[:, j * V7X_LANES:(j + 1) * V7X_LANES]
        y_hi, y_lo = _split_bf16(y)
        wr_hi = wr_hi_ref[...]
        lg_ref[...] = _dot_nt(wr_hi, y_hi) + _dot_nt(wr_hi, y_lo) + _dot_nt(wr_lo_ref[...], y_hi)


def _mix(sinks, q, kv, u, vn_t, h, w_s_t, b_s, g_attn, g_sgu, w_out, g_post, g_ffn, router, *,
         batch, seq):
    T, D = h.shape
    tq = TOKEN_TILE
    attn_w = q.shape[1]
    sgu_w = u.shape[1]
    tiles_per_seq = seq // tq
    blocks_per_seq = seq // BLOCK
    moe = router is not None
    n_groups = w_s_t.shape[0]

    row = lambda b, j: (b * tiles_per_seq + j, 0)
    prev = lambda b, j: (b * blocks_per_seq + jnp.maximum(j * BLOCKS_PER_TILE - 1, 0), 0)
    const2 = lambda b, j: (0, 0)
    const3 = lambda b, j: (0, 0, 0)

    in_specs = [
        pl.BlockSpec(memory_space=pltpu.SMEM),
        pl.BlockSpec((tq, attn_w), row),
        pl.BlockSpec((tq, 4 * V7X_LANES), row),
        pl.BlockSpec((BLOCK, 4 * V7X_LANES), prev),
        pl.BlockSpec((tq, sgu_w), row),
        pl.BlockSpec((sgu_w, tq), lambda b, j: (0, b * tiles_per_seq + j)),
        pl.BlockSpec((tq, D), row),
        pl.BlockSpec((n_groups, BLOCK, BLOCK), const3),
        pl.BlockSpec((n_groups, BLOCK), const2),
        pl.BlockSpec((1, attn_w), const2),
        pl.BlockSpec((1, sgu_w), const2),
        pl.BlockSpec((D, D), const2),
        pl.BlockSpec((1, D), const2),
        pl.BlockSpec((1, D), const2),
    ]
    args = [sinks, q, kv, kv, u, vn_t, h, w_s_t, b_s, g_attn, g_sgu, w_out, g_post, g_ffn]
    out_specs = [pl.BlockSpec((tq, D), row)]
    out_shape = [jax.ShapeDtypeStruct((T, D), F32)]
    blocks = (_nbytes((tq, attn_w), BF16) + 2 * _nbytes((tq, 4 * V7X_LANES), BF16)
              + _nbytes((tq, sgu_w), F32) + _nbytes((tq, sgu_w), BF16) + 2 * _nbytes((tq, D), F32)
              + _nbytes((n_groups, BLOCK, BLOCK), BF16) + _nbytes((D, D), BF16)
              + _nbytes((tq, D), F32))
    if moe:
        n_e = router[0].shape[0]
        in_specs += [pl.BlockSpec((n_e, D), const2), pl.BlockSpec((n_e, D), const2)]
        args += list(router)
        out_specs += [pl.BlockSpec((tq * PK_SUB, V7X_LANES), row),
                      pl.BlockSpec((n_e, tq), lambda b, j: (0, b * tiles_per_seq + j))]
        out_shape += [jax.ShapeDtypeStruct((T * PK_SUB, V7X_LANES), U32),
                      jax.ShapeDtypeStruct((n_e, T), F32)]
    else:
        out_specs += [pl.BlockSpec((tq, D), row)]
        out_shape += [jax.ShapeDtypeStruct((T, D), BF16)]
    scratch = _nbytes((tq, D), BF16)
    temps = 8 * _nbytes((tq, D), F32)
    kern = functools.partial(_mix_kernel, moe=moe, attn_w=attn_w, sgu_w=sgu_w)
    return pl.pallas_call(
        kern,
        grid=(batch, tiles_per_seq),
        in_specs=in_specs,
        out_specs=out_specs,
        out_shape=out_shape,
        scratch_shapes=[pltpu.VMEM((tq, D), BF16)],
        compiler_params=_params(_vmem_limit(blocks, scratch, temps), 2),
        name="mix_moe" if moe else "mix_dense",
    )(*args)


def _dense_ffn_kernel(y_ref, h_ref, wg_ref, wu_ref, wd_ref, g_ref, o_ref, *, chunk):
    y = y_ref[...]
    d_ff = wg_ref.shape[1]
    acc = jnp.zeros(o_ref.shape, F32)
    for c in range(d_ff // chunk):
        sl = slice(c * chunk, (c + 1) * chunk)
        a = _silu(_dot(y, wg_ref[:, sl])) * _dot(y, wu_ref[:, sl])
        acc = acc + _dot(a.astype(BF16), wd_ref[sl, :])
    o_ref[...] = h_ref[...] + _rmsnorm(acc, g_ref[...])


def _dense_ffn(y, h1, w_gate, w_up, w_down, g_post):
    T, D = h1.shape
    d_ff = w_gate.shape[1]
    tm = TOKEN_TILE
    chunk = 2 * V7X_LANES
    assert d_ff % chunk == 0
    row = lambda i: (i, 0)
    const = lambda i: (0, 0)
    blocks = (_nbytes((tm, D), BF16) + 2 * _nbytes((tm, D), F32) + 3 * _nbytes((D, d_ff), BF16))
    temps = 4 * _nbytes((tm, D), F32)
    return pl.pallas_call(
        functools.partial(_dense_ffn_kernel, chunk=chunk),
        grid=(T // tm,),
        in_specs=[
            pl.BlockSpec((tm, D), row),
            pl.BlockSpec((tm, D), row),
            pl.BlockSpec((D, d_ff), const),
            pl.BlockSpec((D, d_ff), const),
            pl.BlockSpec((d_ff, D), const),
            pl.BlockSpec((1, D), const),
        ],
        out_specs=pl.BlockSpec((tm, D), row),
        out_shape=jax.ShapeDtypeStruct((T, D), F32),
        compiler_params=_params(_vmem_limit(blocks, 0, temps), 1),
        name="dense_ffn",
    )(y, h1, w_gate, w_up, w_down, g_post)


def _route_kernel(lg_ref, tri_ref, pos_ref, gate_ref, info_ref, oh_ref, rank_ref):
    n_e, T = lg_ref.shape
    lg = lg_ref[...]
    eidx = lax.broadcasted_iota(I32, (n_e, T), 0)
    m1 = jnp.max(lg, axis=0, keepdims=True)
    i1 = jnp.min(jnp.where(lg == m1, eidx, n_e), axis=0, keepdims=True)
    oh1 = eidx == i1
    lg2 = jnp.where(oh1, -jnp.inf, lg)
    m2 = jnp.max(lg2, axis=0, keepdims=True)
    i2 = jnp.min(jnp.where(lg2 == m2, eidx, n_e), axis=0, keepdims=True)
    oh2 = eidx == i2
    e2 = jnp.exp(m2 - m1)
    gate_ref[0:1, :] = 1.0 / (1.0 + e2)
    gate_ref[1:2, :] = e2 / (1.0 + e2)

    oh_ref[...] = jnp.where(oh1 | oh2, 1.0, 0.0)

    tri = tri_ref[...]

    def body(i, carry):
        start = pl.multiple_of(i * ROUTE_CHUNK, ROUTE_CHUNK)
        ohc = oh_ref[:, pl.ds(start, ROUTE_CHUNK)]
        inc = _dot(ohc.astype(BF16), tri)
        rank_ref[:, pl.ds(start, ROUTE_CHUNK)] = carry + inc - ohc
        return carry + inc[:, ROUTE_CHUNK - 1:ROUTE_CHUNK]

    count = lax.fori_loop(0, T // ROUTE_CHUNK, body, jnp.zeros((n_e, 1), F32))

    tm = np.float32(EXPERT_ROW_TILE)
    n_tiles = jnp.floor((count + (tm - 1.0)) * (1.0 / tm))
    e_col = lax.broadcasted_iota(I32, (n_e, 1), 0)
    first_tile = jnp.zeros((n_e, 1), F32)
    for e in range(n_e - 1):
        nt_e = jnp.sum(jnp.where(e_col == e, n_tiles, 0.0), axis=0, keepdims=True)
        first_tile = first_tile + jnp.where(e_col > e, nt_e, 0.0)
    row0 = first_tile * tm + rank_ref[...]
    pos_ref[0:1, :] = jnp.sum(jnp.where(oh1, row0, 0.0), axis=0, keepdims=True).astype(I32)
    pos_ref[1:2, :] = jnp.sum(jnp.where(oh2, row0, 0.0), axis=0, keepdims=True).astype(I32)

    end_tile = (first_tile + n_tiles).astype(I32)
    tile = lax.broadcasted_iota(I32, (n_e, TILE_INFO_LEN), 1)
    owner = jnp.sum(jnp.where(end_tile <= tile, 1, 0), axis=0, keepdims=True)
    total = jnp.sum(n_tiles, axis=0, keepdims=True).astype(I32)
    lane = lax.broadcasted_iota(I32, (1, TILE_INFO_LEN), 1)
    info_ref[...] = jnp.where(lane == TILE_INFO_LEN - 1, total, jnp.minimum(owner, n_e - 1))


def _route(logits_t, tri):
    n_e, T = logits_t.shape
    blocks = (_nbytes((n_e, T), F32) + _nbytes((ROUTE_CHUNK, ROUTE_CHUNK), BF16)
              + _nbytes((8, T), I32) + _nbytes((8, T), F32))
    scratch = 2 * _nbytes((n_e, T), F32)
    temps = 12 * _nbytes((n_e, T), F32)
    return pl.pallas_call(
        _route_kernel,
        out_shape=[
            jax.ShapeDtypeStruct((TOP_K, T), I32),
            jax.ShapeDtypeStruct((TOP_K, T), F32),
            jax.ShapeDtypeStruct((1, TILE_INFO_LEN), I32),
        ],
        scratch_shapes=[pltpu.VMEM((n_e, T), F32), pltpu.VMEM((n_e, T), F32)],
        compiler_params=pltpu.CompilerParams(vmem_limit_bytes=_vmem_limit(blocks, scratch, temps)),
        name="route",
    )(logits_t, tri)


def _slab_copy(src, src_row, dst, dst_row, sub, sem):
    s0 = pl.multiple_of(src_row * sub, sub)
    d0 = pl.multiple_of(dst_row * sub, sub)
    return pltpu.make_async_copy(src.at[pl.ds(s0, sub)], dst.at[pl.ds(d0, sub)], sem)


def _dispatch_kernel(pos0_ref, pos1_ref, ypk_ref, xs_in_hbm, xs_hbm, sem):
    del xs_in_hbm
    tt = pos0_ref.shape[0]
    pos_refs = (pos0_ref, pos1_ref)

    def issue(t, carry):
        for k in range(TOP_K):
            _slab_copy(ypk_ref, t, xs_hbm, pos_refs[k][t], PK_SUB, sem).start(priority=k)
        return carry

    def drain(t, carry):
        for k in range(TOP_K):
            _slab_copy(ypk_ref, 0, xs_hbm, 0, PK_SUB, sem).wait()
        return carry

    lax.fori_loop(0, tt, issue, 0, unroll=8)
    lax.fori_loop(0, tt, drain, 0, unroll=8)


def _dispatch(pos, ypk, n_rows):
    T = ypk.shape[0] // PK_SUB
    W = V7X_LANES
    tt = DISPATCH_TILE
    xs0 = jnp.zeros((n_rows * PK_SUB, W), U32)
    return pl.pallas_call(
        _dispatch_kernel,
        grid=(T // tt,),
        in_specs=[
            pl.BlockSpec((tt,), lambda i: (i,), memory_space=pltpu.SMEM),
            pl.BlockSpec((tt,), lambda i: (i,), memory_space=pltpu.SMEM),
            pl.BlockSpec((tt * PK_SUB, W), lambda i: (i, 0)),
            pl.BlockSpec(memory_space=pl.ANY),
        ],
        out_specs=pl.BlockSpec(memory_space=pl.ANY),
        out_shape=jax.ShapeDtypeStruct((n_rows * PK_SUB, W), U32),
        scratch_shapes=[pltpu.SemaphoreType.DMA(())],
        input_output_aliases={3: 0},
        compiler_params=pltpu.CompilerParams(
            dimension_semantics=("arbitrary",), has_side_effects=True,
            vmem_limit_bytes=_vmem_limit(_nbytes((tt * PK_SUB, W), U32))),
        name="dispatch",
    )(pos[0], pos[1], ypk, xs0)


def _expert_kernel(info_ref, xs_ref, wg_ref, wu_ref, wd_ref, o_ref, acc_ref):
    i = pl.program_id(0)
    c = pl.program_id(1)
    valid = i < info_ref[TILE_INFO_LEN - 1]
    tm = acc_ref.shape[0]

    @pl.when(c == 0)
    def _():
        acc_ref[...] = jnp.zeros(acc_ref.shape, F32)

    @pl.when(valid)
    def _():
        left, right = _unpack_bf16_pairs([xs_ref[pl.ds(j, tm, stride=PK_SUB), :] for j in range(PK_SUB)])
        x = jnp.concatenate(left + right, axis=1).astype(BF16)
        a = _silu(_dot(x, wg_ref[0])) * _dot(x, wu_ref[0])
        acc_ref[...] += _dot(a.astype(BF16), wd_ref[0])

    @pl.when(c == pl.num_programs(1) - 1)
    def _():
        for j in range(OUT_SUB):
            o_ref[pl.ds(j, tm, stride=OUT_SUB), :] = acc_ref[:, j * V7X_LANES:(j + 1) * V7X_LANES]


def _expert_ffn(tile_info, xs, w_gate, w_up, w_down, n_chunks):
    R = xs.shape[0] // PK_SUB
    W = V7X_LANES
    n_e, D, d_ff = w_gate.shape
    assert D == OUT_SUB * V7X_LANES
    tm = EXPERT_ROW_TILE
    ck = d_ff // n_chunks
    last = TILE_INFO_LEN - 1

    def tile_of(i, info):
        return jnp.minimum(i, info[last] - 1)

    def chunk_of(i, c, info):
        return jnp.where(i < info[last], c, n_chunks - 1)

    grid_spec = pltpu.PrefetchScalarGridSpec(
        num_scalar_prefetch=1,
        grid=(R // tm, n_chunks),
        in_specs=[
            pl.BlockSpec((tm * PK_SUB, W), lambda i, c, info: (tile_of(i, info), 0)),
            pl.BlockSpec((1, D, ck), lambda i, c, info: (info[tile_of(i, info)], 0, chunk_of(i, c, info))),
            pl.BlockSpec((1, D, ck), lambda i, c, info: (info[tile_of(i, info)], 0, chunk_of(i, c, info))),
            pl.BlockSpec((1, ck, D), lambda i, c, info: (info[tile_of(i, info)], chunk_of(i, c, info), 0)),
        ],
        out_specs=pl.BlockSpec((tm * OUT_SUB, V7X_LANES), lambda i, c, info: (i, 0)),
        scratch_shapes=[pltpu.VMEM((tm, D), F32)],
    )
    blocks = (_nbytes((tm * PK_SUB, W), U32) + 3 * _nbytes((D, ck), BF16) + _nbytes((tm, D), F32))
    temps = 3 * _nbytes((tm, ck), F32) + 2 * _nbytes((tm, D), F32)
    return pl.pallas_call(
        _expert_kernel,
        grid_spec=grid_spec,
        out_shape=jax.ShapeDtypeStruct((R * OUT_SUB, V7X_LANES), F32),
        compiler_params=_params(_vmem_limit(blocks, _nbytes((tm, D), F32), temps), 2),
        name="expert_ffn",
    )(tile_info, xs, w_gate, w_up, w_down)


def _combine_kernel(pos0_ref, pos1_ref, gate_ref, h_ref, g_ref, os_hbm, o_ref, a_ref, b_ref, sem):
    tt = h_ref.shape[0]
    bufs = (a_ref, b_ref)
    pos_refs = (pos0_ref, pos1_ref)

    def issue(t, carry):
        for k in range(TOP_K):
            _slab_copy(os_hbm, pos_refs[k][t], bufs[k], t, OUT_SUB, sem).start(priority=k)
        return carry

    def drain(t, carry):
        for k in range(TOP_K):
            _slab_copy(os_hbm, 0, bufs[k], 0, OUT_SUB, sem).wait()
        return carry

    lax.fori_loop(0, tt, issue, 0, unroll=8)
    lax.fori_loop(0, tt, drain, 0, unroll=8)
    rows = lambda ref: jnp.concatenate(
        [ref[pl.ds(j, tt, stride=OUT_SUB), :] for j in range(OUT_SUB)], axis=1)
    f = gate_ref[:, 0:1] * rows(a_ref) + gate_ref[:, 1:2] * rows(b_ref)
    o_ref[...] = h_ref[...] + _rmsnorm(f, g_ref[...])


def _combine(pos, gate_t, h1, g_post, o_sorted):
    T, D = h1.shape
    tt = COMBINE_TILE
    row = lambda i: (i, 0)
    blocks = (_nbytes((tt, V7X_LANES), F32) + 2 * _nbytes((tt, D), F32))
    scratch = 2 * _nbytes((tt, D), F32)
    temps = 3 * _nbytes((tt, D), F32)
    return pl.pallas_call(
        _combine_kernel,
        grid=(T // tt,),
        in_specs=[
            pl.BlockSpec((tt,), lambda i: (i,), memory_space=pltpu.SMEM),
            pl.BlockSpec((tt,), lambda i: (i,), memory_space=pltpu.SMEM),
            pl.BlockSpec((tt, TOP_K), row),
            pl.BlockSpec((tt, D), row),
            pl.BlockSpec((1, D), lambda i: (0, 0)),
            pl.BlockSpec(memory_space=pl.ANY),
        ],
        out_specs=pl.BlockSpec((tt, D), row),
        out_shape=jax.ShapeDtypeStruct((T, D), F32),
        scratch_shapes=[pltpu.VMEM((tt * OUT_SUB, V7X_LANES), F32),
                        pltpu.VMEM((tt * OUT_SUB, V7X_LANES), F32),
                        pltpu.SemaphoreType.DMA(())],
        compiler_params=_params(_vmem_limit(blocks, scratch, temps), 1),
        name="combine",
    )(pos[0], pos[1], gate_t, h1, g_post, o_sorted)


def _moe_ffn(ypk, logits_t, h1, g_post, w_gate, w_up, w_down, tri):
    T = h1.shape[0]
    n_tiles_max = (TOP_K * T) // EXPERT_ROW_TILE + N_EXPERTS
    assert n_tiles_max < TILE_INFO_LEN
    pos, gate, info = _route(logits_t, tri)
    xs = _dispatch(pos, ypk, n_tiles_max * EXPERT_ROW_TILE)
    o_sorted = _expert_ffn(info.reshape(TILE_INFO_LEN), xs, w_gate, w_up, w_down, n_chunks=2)
    return _combine(pos, gate.T, h1, g_post, o_sorted)


def _rope_inv_freq_lanes():
    lane = np.arange(V7X_LANES) % HEAD_DIM
    half = ROPE_DIM // 2
    idx = np.where(lane < half, lane, lane - half)
    inv = np.power(np.float32(ROPE_THETA), -(2.0 * idx.astype(np.float32)) / np.float32(ROPE_DIM))
    return jnp.asarray(np.where(lane < ROPE_DIM, inv, 0.0).astype(np.float32)).reshape(1, V7X_LANES)


def kernel(x, positions, pre_mix_norm, w_in, attn_sinks, sgu_ln_g, sgu_ln_b, sgu_w, sgu_b, attn_out_norm, sgu_out_norm, w_out, post_mix_norm, pre_ffn_norm, post_ffn_norm, ffn_w_gate, ffn_w_up, ffn_w_down, moe_router, moe_w_gate, moe_w_up, moe_w_down):
    B, S, D = x.shape
    T = B * S
    depth = pre_mix_norm.shape[0]
    n_q_heads = attn_sinks.shape[1]
    attn_w = n_q_heads * HEAD_DIM
    kv_w = (n_q_heads // Q_PER_KV) * HEAD_DIM
    sgu_width = sgu_ln_g.shape[1]
    assert kv_w == V7X_LANES and attn_w % V7X_LANES == 0 and sgu_width % V7X_LANES == 0
    assert S % TOKEN_TILE == 0 and T % DISPATCH_TILE == 0 and sgu_w.shape[2] == BLOCK
    assert moe_router.shape[-1] == N_EXPERTS

    invf = _rope_inv_freq_lanes()
    main_w = attn_w + 2 * kv_w + sgu_width
    lane_rep = lambda a: jnp.broadcast_to(a[:, None], (a.shape[0], V7X_LANES))
    tri = jnp.asarray(np.triu(np.ones((ROUTE_CHUNK, ROUTE_CHUNK), np.float32)), BF16)
    row2 = lambda a: a.reshape(1, -1)

    h = x.reshape(T, D)
    cos, sin = _rope_tables(positions.reshape(T, 1), invf)
    for layer in range(depth):
        w_l = w_in[layer].astype(BF16)
        q, kv, u, vn_t = _in_proj(
            h, cos, sin, row2(pre_mix_norm[layer]), w_l[:, :main_w], w_l[:, main_w:].T,
            lane_rep(sgu_ln_g[layer]), lane_rep(sgu_ln_b[layer]),
            attn_w=attn_w, kv_w=kv_w, sgu_w=sgu_width)
        is_moe = layer % 2 == 1
        i = layer // 2
        router = None
        if is_moe:
            wr = moe_router[i].T
            wr_hi = wr.astype(BF16)
            router = (wr_hi, (wr - wr_hi.astype(F32)).astype(BF16))
        outs = _mix(
            attn_sinks[layer], q, kv, u, vn_t, h,
            jnp.swapaxes(sgu_w[layer], 1, 2).astype(BF16), sgu_b[layer],
            row2(attn_out_norm[layer]), row2(sgu_out_norm[layer]), w_out[layer].astype(BF16),
            row2(post_mix_norm[layer]), row2(pre_ffn_norm[layer]), router, batch=B, seq=S)
        g_post = row2(post_ffn_norm[layer])
        if is_moe:
            h1, ypk, logits_t = outs
            h = _moe_ffn(ypk, logits_t, h1, g_post, moe_w_gate[i].astype(BF16),
                         moe_w_up[i].astype(BF16), moe_w_down[i].astype(BF16), tri)
        else:
            h1, y = outs
            h = _dense_ffn(y, h1, ffn_w_gate[i].astype(BF16), ffn_w_up[i].astype(BF16),
                           ffn_w_down[i].astype(BF16), g_post)
    return h.reshape(B, S, D)
```

```python
import functools

import numpy as np
import jax
import jax.numpy as jnp
from jax import lax
from jax.experimental import pallas as pl
from jax.experimental.pallas import tpu as pltpu

F32 = jnp.float32
BF16 = jnp.bfloat16
I32 = jnp.int32

V7X_LANES = 128
V7X_SUBLANES = 8
V7X_VMEM_BUDGET_BYTES = 56 * 1024 * 1024

HEAD_DIM = 64
Q_PER_KV = 4
WINDOW = 128
BLOCK = 128
ROPE_DIM = HEAD_DIM // 4
ROPE_THETA = 500000.0
N_EXPERTS = 8
TOP_K = 2
EPS = 1e-6
NEG_BIG = -1e30

TOKEN_TILE = 512
BLOCKS_PER_TILE = TOKEN_TILE // BLOCK
IN_PROJ_SUBTILE = 256
MIX_OUT_SUBTILE = 256
EXPERT_ROW_TILE = 512
ROUTE_CHUNK = 512
DISPATCH_TILE = 1024
COMBINE_TILE = 512
TILE_INFO_LEN = 256
IN_SUB = 8
OUT_SUB = 8


def _vmem_limit(block_bytes, scratch_bytes=0, temp_bytes=0):
    need = 2 * int(block_bytes) + int(scratch_bytes) + int(temp_bytes)
    return int(min(max(need, 16 * 1024 * 1024), V7X_VMEM_BUDGET_BYTES))


def _nbytes(shape, dtype):
    return int(np.prod(shape)) * jnp.dtype(dtype).itemsize


def _params(vmem_bytes, n_grid_dims):
    return pltpu.CompilerParams(
        dimension_semantics=("arbitrary",) * n_grid_dims,
        vmem_limit_bytes=vmem_bytes,
    )


def _rmsnorm(x, g):
    return x * lax.rsqrt(jnp.mean(x * x, axis=-1, keepdims=True) + EPS) * g


def _gelu_tanh(x):
    c = np.float32(np.sqrt(2.0 / np.pi))
    return x * (0.5 * (1.0 + jnp.tanh(c * (x + 0.044715 * (x * x * x)))))


def _silu(x):
    return x * (1.0 / (1.0 + jnp.exp(-x)))


def _split_bf16(x):
    hi = x.astype(BF16)
    lo = (x - hi.astype(F32)).astype(BF16)
    return hi, lo


def _dot(a, b):
    return jnp.dot(a, b, preferred_element_type=F32)


def _dot_nt(a, b):
    return lax.dot_general(a, b, (((1,), (1,)), ((), ())), preferred_element_type=F32)


def _rope_table_kernel(pos_ref, invf_ref, cos_ref, sin_ref):
    ang = pos_ref[...].astype(F32) * invf_ref[...]
    cos_ref[...] = jnp.cos(ang)
    sin_ref[...] = jnp.sin(ang)


def _rope_tables(pos, invf):
    T = pos.shape[0]
    tm = TOKEN_TILE
    row = lambda i: (i, 0)
    blocks = 3 * _nbytes((tm, V7X_LANES), F32)
    return pl.pallas_call(
        _rope_table_kernel,
        grid=(T // tm,),
        in_specs=[pl.BlockSpec((tm, 1), row), pl.BlockSpec((1, V7X_LANES), lambda i: (0, 0))],
        out_specs=[pl.BlockSpec((tm, V7X_LANES), row), pl.BlockSpec((tm, V7X_LANES), row)],
        out_shape=[jax.ShapeDtypeStruct((T, V7X_LANES), F32), jax.ShapeDtypeStruct((T, V7X_LANES), F32)],
        compiler_params=_params(_vmem_limit(blocks, 0, 8 * blocks), 1),
        name="rope_tables",
    )(pos, invf)


def _in_proj_kernel(h_ref, cos_ref, sin_ref, g_ref, w_ref, wvt_ref, lng_ref, lnb_ref,
                    q_ref, kv_ref, u_ref, vnt_ref, *, attn_w, kv_w, sgu_w):
    lane = lax.broadcasted_iota(I32, (1, V7X_LANES), 1)
    hl = lane % HEAD_DIM
    low = lane < HEAD_DIM
    half = ROPE_DIM // 2
    scale = np.float32(HEAD_DIM ** -0.5)
    tm = h_ref.shape[0]

    for r0 in range(0, tm, IN_PROJ_SUBTILE):
        rs = slice(r0, r0 + IN_PROJ_SUBTILE)
        hn = _rmsnorm(h_ref[rs, :], g_ref[...]).astype(BF16)
        z = _dot(hn, w_ref[...])

        cos = cos_ref[rs, :]
        sin = sin_ref[rs, :]
        sin_a = jnp.where(hl < half, -sin, 0.0)
        sin_b = jnp.where((hl >= half) & (hl < ROPE_DIM), sin, 0.0)

        def rope(t):
            return (t * cos + pltpu.roll(t, V7X_LANES - half, 1) * sin_a
                    + pltpu.roll(t, half, 1) * sin_b)

        for c in range(attn_w // V7X_LANES):
            sl = slice(c * V7X_LANES, (c + 1) * V7X_LANES)
            q_ref[rs, sl] = (rope(z[:, sl]) * scale).astype(BF16)

        k = rope(z[:, attn_w:attn_w + kv_w])
        v = z[:, attn_w + kv_w:attn_w + 2 * kv_w]
        kr = pltpu.roll(k, HEAD_DIM, 1)
        vr = pltpu.roll(v, HEAD_DIM, 1)
        kv_ref[rs, 0 * V7X_LANES:1 * V7X_LANES] = jnp.where(low, k, kr).astype(BF16)
        kv_ref[rs, 1 * V7X_LANES:2 * V7X_LANES] = jnp.where(low, kr, k).astype(BF16)
        kv_ref[rs, 2 * V7X_LANES:3 * V7X_LANES] = jnp.where(low, v, vr).astype(BF16)
        kv_ref[rs, 3 * V7X_LANES:4 * V7X_LANES] = jnp.where(low, vr, v).astype(BF16)

        o = attn_w + 2 * kv_w
        u_ref[rs, :] = _gelu_tanh(z[:, o:o + sgu_w])

        zvt = _gelu_tanh(_dot_nt(wvt_ref[...], hn))
        g3 = zvt.reshape(sgu_w // HEAD_DIM, HEAD_DIM, IN_PROJ_SUBTILE)
        d = g3 - jnp.mean(g3, axis=1, keepdims=True)
        var = jnp.mean(d * d, axis=1, keepdims=True)
        vn = (d * lax.rsqrt(var + EPS)).reshape(sgu_w, IN_PROJ_SUBTILE)
        for c in range(IN_PROJ_SUBTILE // V7X_LANES):
            sl = slice(c * V7X_LANES, (c + 1) * V7X_LANES)
            vnt_ref[:, r0 + c * V7X_LANES:r0 + (c + 1) * V7X_LANES] = (
                vn[:, sl] * lng_ref[...] + lnb_ref[...]).astype(BF16)


def _in_proj(h, cos, sin, g_pre, w_main, w_zv_t, ln_g, ln_b, *, attn_w, kv_w, sgu_w):
    T, D = h.shape
    tm = TOKEN_TILE
    in_w = w_main.shape[1] + sgu_w
    row = lambda i: (i, 0)
    const = lambda i: (0, 0)
    blocks = (_nbytes((tm, D), F32) + 2 * _nbytes((tm, V7X_LANES), F32) + _nbytes((D, in_w), BF16)
              + _nbytes((tm, attn_w), BF16) + _nbytes((tm, 4 * V7X_LANES), BF16)
              + _nbytes((tm, sgu_w), F32) + _nbytes((tm, sgu_w), BF16))
    temps = 6 * _nbytes((tm, in_w), F32)
    kern = functools.partial(_in_proj_kernel, attn_w=attn_w, kv_w=kv_w, sgu_w=sgu_w)
    return pl.pallas_call(
        kern,
        grid=(T // tm,),
        in_specs=[
            pl.BlockSpec((tm, D), row),
            pl.BlockSpec((tm, V7X_LANES), row),
            pl.BlockSpec((tm, V7X_LANES), row),
            pl.BlockSpec((1, D), const),
            pl.BlockSpec((D, in_w - sgu_w), const),
            pl.BlockSpec((sgu_w, D), const),
            pl.BlockSpec((sgu_w, V7X_LANES), const),
            pl.BlockSpec((sgu_w, V7X_LANES), const),
        ],
        out_specs=[
            pl.BlockSpec((tm, attn_w), row),
            pl.BlockSpec((tm, 4 * V7X_LANES), row),
            pl.BlockSpec((tm, sgu_w), row),
            pl.BlockSpec((sgu_w, tm), lambda i: (0, i)),
        ],
        out_shape=[
            jax.ShapeDtypeStruct((T, attn_w), BF16),
            jax.ShapeDtypeStruct((T, 4 * V7X_LANES), BF16),
            jax.ShapeDtypeStruct((T, sgu_w), F32),
            jax.ShapeDtypeStruct((sgu_w, T), BF16),
        ],
        compiler_params=_params(_vmem_limit(blocks, 0, temps), 1),
        name="in_proj",
    )(h, cos, sin, g_pre, w_main, w_zv_t, ln_g, ln_b)


def _mix_kernel(sink_ref, q_ref, kv_ref, kvp_ref, u_ref, vnt_ref, h_ref, wst_ref, bs_ref,
                ga_ref, gs_ref, wo_ref, gpost_ref, gffn_ref, *rest, moe, attn_w, sgu_w):
    if moe:
        wr_hi_ref, wr_lo_ref, h1_ref, yrow_ref, lg_ref, merged_ref = rest
    else:
        h1_ref, y_ref, merged_ref = rest

    first_tile = pl.program_id(1) == 0
    n_pairs = attn_w // V7X_LANES
    band = 2 * BLOCK

    n_kvh = n_pairs // 2
    row2 = lax.broadcasted_iota(I32, (2 * BLOCK, 2 * band), 0)
    row = row2 % BLOCK
    col = lax.broadcasted_iota(I32, (2 * BLOCK, 2 * band), 1) % band
    in_window = (col > row) & (col <= row + WINDOW)
    upper = lax.broadcasted_iota(I32, (2 * BLOCK, 1), 0) < BLOCK
    lane = lax.broadcasted_iota(I32, (1, V7X_LANES), 1)
    low = lane < HEAD_DIM
    zero_bf = jnp.zeros((), BF16)
    n_chunks = sgu_w // V7X_LANES

    causal_t = (lax.broadcasted_iota(I32, (BLOCK, BLOCK), 0)
                <= lax.broadcasted_iota(I32, (BLOCK, BLOCK), 1))
    top = lax.broadcasted_iota(I32, (2 * HEAD_DIM, 1), 0) < HEAD_DIM
    mixed_t = []
    for g in range(2 * n_chunks):
        lhs = jnp.concatenate(
            [vnt_ref[g * HEAD_DIM:(g + 1) * HEAD_DIM, b * BLOCK:(b + 1) * BLOCK]
             for b in range(BLOCKS_PER_TILE)], axis=0)
        mixed_t.append(_dot(lhs, jnp.where(causal_t, wst_ref[g], zero_bf)))
    bias_t = [jnp.where(top, bs_ref[2 * c:2 * c + 1, :], bs_ref[2 * c + 1:2 * c + 2, :])
              for c in range(n_chunks)]

    for blk in range(BLOCKS_PER_TILE):
        rows = slice(blk * BLOCK, (blk + 1) * BLOCK)
        if blk == 0:
            kv_prev = kvp_ref[...]
            mask = in_window & ((col >= BLOCK) | jnp.logical_not(first_tile))
        else:
            kv_prev = kv_ref[(blk - 1) * BLOCK:blk * BLOCK, :]
            mask = in_window
        kv_band = jnp.concatenate([kv_prev, kv_ref[rows, :]], axis=0)

        attn_parts = [None] * n_pairs
        for kvh in range(n_kvh):
            kk = kv_band[:, kvh * V7X_LANES:(kvh + 1) * V7X_LANES]
            vv = kv_band[:, (n_kvh + kvh) * V7X_LANES:(n_kvh + 1 + kvh) * V7X_LANES]
            k2 = jnp.concatenate([jnp.where(low, kk, zero_bf), jnp.where(low, zero_bf, kk)], axis=0)
            v2 = jnp.concatenate([jnp.where(low, vv, zero_bf), jnp.where(low, zero_bf, vv)], axis=0)
            p0, p1 = 2 * kvh, 2 * kvh + 1
            qq = jnp.concatenate([q_ref[rows, p0 * V7X_LANES:(p0 + 1) * V7X_LANES],
                                  q_ref[rows, p1 * V7X_LANES:(p1 + 1) * V7X_LANES]], axis=0)
            s = jnp.where(mask, _dot_nt(qq, k2), NEG_BIG)
            probs, inv = [], []
            for hh in range(2):
                sink = jnp.where(upper, sink_ref[2 * p0 + hh], sink_ref[2 * p1 + hh])
                sh = s[:, hh * band:(hh + 1) * band]
                m = jnp.maximum(jnp.max(sh, axis=-1, keepdims=True), sink)
                p = jnp.exp(sh - m)
                denom = jnp.sum(p, axis=-1, keepdims=True) + jnp.exp(sink - m)
                probs.append(p.astype(BF16))
                inv.append(1.0 / denom)
            o = _dot(jnp.concatenate(probs, axis=1), v2) * jnp.where(low, inv[0], inv[1])
            attn_parts[p0] = o[:BLOCK]
            attn_parts[p1] = o[BLOCK:]
        attn = jnp.concatenate(attn_parts, axis=1)

        sgu_parts = []
        for c in range(n_chunks):
            drows = slice(blk * HEAD_DIM, (blk + 1) * HEAD_DIM)
            pair_t = jnp.concatenate([mixed_t[2 * c][drows], mixed_t[2 * c + 1][drows]], axis=0)
            sgu_parts.append((pair_t + bias_t[c]).T)
        sgu = u_ref[rows, :] * jnp.concatenate(sgu_parts, axis=1)

        merged_ref[rows, 0:attn_w] = _rmsnorm(attn, ga_ref[...]).astype(BF16)
        merged_ref[rows, attn_w:attn_w + sgu_w] = _rmsnorm(sgu, gs_ref[...]).astype(BF16)

    ys = []
    for r0 in range(0, h_ref.shape[0], MIX_OUT_SUBTILE):
        rs = slice(r0, r0 + MIX_OUT_SUBTILE)
        proj = _dot(merged_ref[rs, :], wo_ref[...])
        h1 = h_ref[rs, :] + _rmsnorm(proj, gpost_ref[...])
        h1_ref[rs, :] = h1
        y = _rmsnorm(h1, gffn_ref[...])
        if not moe:
            y_ref[rs, :] = y.astype(BF16)
        else:
            ys.append(y)
    if moe:
        y = jnp.concatenate(ys, axis=0)
        for j in range(IN_SUB):
            yrow_ref[pl.ds(j, y.shape[0], stride=IN_SUB), :] = y[:, j * V7X_LANES:(j + 1) * V7X_LANES]
        y_hi, y_lo = _split_bf16(y)
        wr_hi = wr_hi_ref[...]
        lg_ref[...] = _dot_nt(wr_hi, y_hi) + _dot_nt(wr_hi, y_lo) + _dot_nt(wr_lo_ref[...], y_hi)


def _mix(sinks, q, kv, u, vn_t, h, w_s_t, b_s, g_attn, g_sgu, w_out, g_post, g_ffn, router, *,
         batch, seq):
    T, D = h.shape
    tq = TOKEN_TILE
    attn_w = q.shape[1]
    sgu_w = u.shape[1]
    tiles_per_seq = seq // tq
    blocks_per_seq = seq // BLOCK
    moe = router is not None
    n_groups = w_s_t.shape[0]

    row = lambda b, j: (b * tiles_per_seq + j, 0)
    prev = lambda b, j: (b * blocks_per_seq + jnp.maximum(j * BLOCKS_PER_TILE - 1, 0), 0)
    const2 = lambda b, j: (0, 0)
    const3 = lambda b, j: (0, 0, 0)

    in_specs = [
        pl.BlockSpec(memory_space=pltpu.SMEM),
        pl.BlockSpec((tq, attn_w), row),
        pl.BlockSpec((tq, 4 * V7X_LANES), row),
        pl.BlockSpec((BLOCK, 4 * V7X_LANES), prev),
        pl.BlockSpec((tq, sgu_w), row),
        pl.BlockSpec((sgu_w, tq), lambda b, j: (0, b * tiles_per_seq + j)),
        pl.BlockSpec((tq, D), row),
        pl.BlockSpec((n_groups, BLOCK, BLOCK), const3),
        pl.BlockSpec((n_groups, BLOCK), const2),
        pl.BlockSpec((1, attn_w), const2),
        pl.BlockSpec((1, sgu_w), const2),
        pl.BlockSpec((D, D), const2),
        pl.BlockSpec((1, D), const2),
        pl.BlockSpec((1, D), const2),
    ]
    args = [sinks, q, kv, kv, u, vn_t, h, w_s_t, b_s, g_attn, g_sgu, w_out, g_post, g_ffn]
    out_specs = [pl.BlockSpec((tq, D), row)]
    out_shape = [jax.ShapeDtypeStruct((T, D), F32)]
    blocks = (_nbytes((tq, attn_w), BF16) + 2 * _nbytes((tq, 4 * V7X_LANES), BF16)
              + _nbytes((tq, sgu_w), F32) + _nbytes((tq, sgu_w), BF16) + 2 * _nbytes((tq, D), F32)
              + _nbytes((n_groups, BLOCK, BLOCK), BF16) + _nbytes((D, D), BF16)
              + _nbytes((tq, D), F32))
    if moe:
        n_e = router[0].shape[0]
        in_specs += [pl.BlockSpec((n_e, D), const2), pl.BlockSpec((n_e, D), const2)]
        args += list(router)
        out_specs += [pl.BlockSpec((tq * IN_SUB, V7X_LANES), row),
                      pl.BlockSpec((n_e, tq), lambda b, j: (0, b * tiles_per_seq + j))]
        out_shape += [jax.ShapeDtypeStruct((T * IN_SUB, V7X_LANES), F32),
                      jax.ShapeDtypeStruct((n_e, T), F32)]
    else:
        out_specs += [pl.BlockSpec((tq, D), row)]
        out_shape += [jax.ShapeDtypeStruct((T, D), BF16)]
    scratch = _nbytes((tq, D), BF16)
    temps = 8 * _nbytes((tq, D), F32)
    kern = functools.partial(_mix_kernel, moe=moe, attn_w=attn_w, sgu_w=sgu_w)
    return pl.pallas_call(
        kern,
        grid=(batch, tiles_per_seq),
        in_specs=in_specs,
        out_specs=out_specs,
        out_shape=out_shape,
        scratch_shapes=[pltpu.VMEM((tq, D), BF16)],
        compiler_params=_params(_vmem_limit(blocks, scratch, temps), 2),
        name="mix_moe" if moe else "mix_dense",
    )(*args)


def _dense_ffn_kernel(y_ref, h_ref, wg_ref, wu_ref, wd_ref, g_ref, o_ref, *, chunk):
    y = y_ref[...]
    d_ff = wg_ref.shape[1]
    acc = jnp.zeros(o_ref.shape, F32)
    for c in range(d_ff // chunk):
        sl = slice(c * chunk, (c + 1) * chunk)
        a = _silu(_dot(y, wg_ref[:, sl])) * _dot(y, wu_ref[:, sl])
        acc = acc + _dot(a.astype(BF16), wd_ref[sl, :])
    o_ref[...] = h_ref[...] + _rmsnorm(acc, g_ref[...])


def _dense_ffn(y, h1, w_gate, w_up, w_down, g_post):
    T, D = h1.shape
    d_ff = w_gate.shape[1]
    tm = TOKEN_TILE
    chunk = 2 * V7X_LANES
    assert d_ff % chunk == 0
    row = lambda i: (i, 0)
    const = lambda i: (0, 0)
    blocks = (_nbytes((tm, D), BF16) + 2 * _nbytes((tm, D), F32) + 3 * _nbytes((D, d_ff), BF16))
    temps = 4 * _nbytes((tm, D), F32)
    return pl.pallas_call(
        functools.partial(_dense_ffn_kernel, chunk=chunk),
        grid=(T // tm,),
        in_specs=[
            pl.BlockSpec((tm, D), row),
            pl.BlockSpec((tm, D), row),
            pl.BlockSpec((D, d_ff), const),
            pl.BlockSpec((D, d_ff), const),
            pl.BlockSpec((d_ff, D), const),
            pl.BlockSpec((1, D), const),
        ],
        out_specs=pl.BlockSpec((tm, D), row),
        out_shape=jax.ShapeDtypeStruct((T, D), F32),
        compiler_params=_params(_vmem_limit(blocks, 0, temps), 1),
        name="dense_ffn",
    )(y, h1, w_gate, w_up, w_down, g_post)


def _route_kernel(lg_ref, tri_ref, pos_ref, gate_ref, info_ref, oh_ref, rank_ref):
    n_e, T = lg_ref.shape
    lg = lg_ref[...]
    eidx = lax.broadcasted_iota(I32, (n_e, T), 0)
    m1 = jnp.max(lg, axis=0, keepdims=True)
    i1 = jnp.min(jnp.where(lg == m1, eidx, n_e), axis=0, keepdims=True)
    oh1 = eidx == i1
    lg2 = jnp.where(oh1, -jnp.inf, lg)
    m2 = jnp.max(lg2, axis=0, keepdims=True)
    i2 = jnp.min(jnp.where(lg2 == m2, eidx, n_e), axis=0, keepdims=True)
    oh2 = eidx == i2
    e2 = jnp.exp(m2 - m1)
    gate_ref[0:1, :] = 1.0 / (1.0 + e2)
    gate_ref[1:2, :] = e2 / (1.0 + e2)

    oh_ref[...] = jnp.where(oh1 | oh2, 1.0, 0.0)

    tri = tri_ref[...]

    def body(i, carry):
        start = pl.multiple_of(i * ROUTE_CHUNK, ROUTE_CHUNK)
        ohc = oh_ref[:, pl.ds(start, ROUTE_CHUNK)]
        inc = _dot(ohc.astype(BF16), tri)
        rank_ref[:, pl.ds(start, ROUTE_CHUNK)] = carry + inc - ohc
        return carry + inc[:, ROUTE_CHUNK - 1:ROUTE_CHUNK]

    count = lax.fori_loop(0, T // ROUTE_CHUNK, body, jnp.zeros((n_e, 1), F32))

    tm = np.float32(EXPERT_ROW_TILE)
    n_tiles = jnp.floor((count + (tm - 1.0)) * (1.0 / tm))
    e_col = lax.broadcasted_iota(I32, (n_e, 1), 0)
    first_tile = jnp.zeros((n_e, 1), F32)
    for e in range(n_e - 1):
        nt_e = jnp.sum(jnp.where(e_col == e, n_tiles, 0.0), axis=0, keepdims=True)
        first_tile = first_tile + jnp.where(e_col > e, nt_e, 0.0)
    row0 = first_tile * tm + rank_ref[...]
    pos_ref[0:1, :] = jnp.sum(jnp.where(oh1, row0, 0.0), axis=0, keepdims=True).astype(I32)
    pos_ref[1:2, :] = jnp.sum(jnp.where(oh2, row0, 0.0), axis=0, keepdims=True).astype(I32)

    end_tile = (first_tile + n_tiles).astype(I32)
    tile = lax.broadcasted_iota(I32, (n_e, TILE_INFO_LEN), 1)
    owner = jnp.sum(jnp.where(end_tile <= tile, 1, 0), axis=0, keepdims=True)
    total = jnp.sum(n_tiles, axis=0, keepdims=True).astype(I32)
    lane = lax.broadcasted_iota(I32, (1, TILE_INFO_LEN), 1)
    info_ref[...] = jnp.where(lane == TILE_INFO_LEN - 1, total, jnp.minimum(owner, n_e - 1))


def _route(logits_t, tri):
    n_e, T = logits_t.shape
    blocks = (_nbytes((n_e, T), F32) + _nbytes((ROUTE_CHUNK, ROUTE_CHUNK), BF16)
              + _nbytes((8, T), I32) + _nbytes((8, T), F32))
    scratch = 2 * _nbytes((n_e, T), F32)
    temps = 12 * _nbytes((n_e, T), F32)
    return pl.pallas_call(
        _route_kernel,
        out_shape=[
            jax.ShapeDtypeStruct((TOP_K, T), I32),
            jax.ShapeDtypeStruct((TOP_K, T), F32),
            jax.ShapeDtypeStruct((1, TILE_INFO_LEN), I32),
        ],
        scratch_shapes=[pltpu.VMEM((n_e, T), F32), pltpu.VMEM((n_e, T), F32)],
        compiler_params=pltpu.CompilerParams(vmem_limit_bytes=_vmem_limit(blocks, scratch, temps)),
        name="route",
    )(logits_t, tri)


def _slab_copy(src, src_row, dst, dst_row, sub, sem):
    s0 = pl.multiple_of(src_row * sub, sub)
    d0 = pl.multiple_of(dst_row * sub, sub)
    return pltpu.make_async_copy(src.at[pl.ds(s0, sub)], dst.at[pl.ds(d0, sub)], sem)


def _dispatch_kernel(pos0_ref, pos1_ref, yrow_ref, xs_in_hbm, xs_hbm, sem):
    del xs_in_hbm
    tt = pos0_ref.shape[0]
    pos_refs = (pos0_ref, pos1_ref)

    def issue(t, carry):
        for k in range(TOP_K):
            _slab_copy(yrow_ref, t, xs_hbm, pos_refs[k][t], IN_SUB, sem).start(priority=k)
        return carry

    def drain(t, carry):
        for k in range(TOP_K):
            _slab_copy(yrow_ref, 0, xs_hbm, 0, IN_SUB, sem).wait()
        return carry

    lax.fori_loop(0, tt, issue, 0, unroll=8)
    lax.fori_loop(0, tt, drain, 0, unroll=8)


def _dispatch(pos, yrow, n_rows):
    T = yrow.shape[0] // IN_SUB
    W = V7X_LANES
    tt = DISPATCH_TILE
    xs0 = jnp.zeros((n_rows * IN_SUB, W), F32)
    return pl.pallas_call(
        _dispatch_kernel,
        grid=(T // tt,),
        in_specs=[
            pl.BlockSpec((tt,), lambda i: (i,), memory_space=pltpu.SMEM),
            pl.BlockSpec((tt,), lambda i: (i,), memory_space=pltpu.SMEM),
            pl.BlockSpec((tt * IN_SUB, W), lambda i: (i, 0)),
            pl.BlockSpec(memory_space=pl.ANY),
        ],
        out_specs=pl.BlockSpec(memory_space=pl.ANY),
        out_shape=jax.ShapeDtypeStruct((n_rows * IN_SUB, W), F32),
        scratch_shapes=[pltpu.SemaphoreType.DMA(())],
        input_output_aliases={3: 0},
        compiler_params=pltpu.CompilerParams(
            dimension_semantics=("arbitrary",), has_side_effects=True,
            vmem_limit_bytes=_vmem_limit(_nbytes((tt * IN_SUB, W), F32))),
        name="dispatch",
    )(pos[0], pos[1], yrow, xs0)


def _expert_kernel(info_ref, xs_ref, wg_ref, wu_ref, wd_ref, o_ref, acc_ref, *, n_chunks):
    i = pl.program_id(0)
    c = pl.program_id(1)
    valid = i < info_ref[TILE_INFO_LEN - 1]
    last = c == pl.num_programs(1) - 1
    tm = acc_ref.shape[0]

    def chunk_out():
        x = jnp.concatenate([xs_ref[pl.ds(j, tm, stride=IN_SUB), :] for j in range(IN_SUB)],
                            axis=1).astype(BF16)
        a = _silu(_dot(x, wg_ref[0])) * _dot(x, wu_ref[0])
        return _dot(a.astype(BF16), wd_ref[0])

    def write_rows(res):
        for j in range(OUT_SUB):
            o_ref[pl.ds(j, tm, stride=OUT_SUB), :] = res[:, j * V7X_LANES:(j + 1) * V7X_LANES]

    @pl.when(valid & (c == 0))
    def _():
        acc_ref[...] = chunk_out()

    if n_chunks > 2:
        @pl.when(valid & (c > 0) & jnp.logical_not(last))
        def _():
            acc_ref[...] += chunk_out()

    @pl.when(valid & last)
    def _():
        write_rows(acc_ref[...] + chunk_out())

    @pl.when(jnp.logical_not(valid) & last)
    def _():
        write_rows(jnp.zeros(acc_ref.shape, F32))


def _expert_ffn(tile_info, xs, w_gate, w_up, w_down, n_chunks):
    R = xs.shape[0] // IN_SUB
    W = V7X_LANES
    n_e, D, d_ff = w_gate.shape
    assert D == OUT_SUB * V7X_LANES
    tm = EXPERT_ROW_TILE
    ck = d_ff // n_chunks
    last = TILE_INFO_LEN - 1

    def tile_of(i, info):
        return jnp.minimum(i, info[last] - 1)

    def chunk_of(i, c, info):
        return jnp.where(i < info[last], c, n_chunks - 1)

    grid_spec = pltpu.PrefetchScalarGridSpec(
        num_scalar_prefetch=1,
        grid=(R // tm, n_chunks),
        in_specs=[
            pl.BlockSpec((tm * IN_SUB, W), lambda i, c, info: (tile_of(i, info), 0)),
            pl.BlockSpec((1, D, ck), lambda i, c, info: (info[tile_of(i, info)], 0, chunk_of(i, c, info))),
            pl.BlockSpec((1, D, ck), lambda i, c, info: (info[tile_of(i, info)], 0, chunk_of(i, c, info))),
            pl.BlockSpec((1, ck, D), lambda i, c, info: (info[tile_of(i, info)], chunk_of(i, c, info), 0)),
        ],
        out_specs=pl.BlockSpec((tm * OUT_SUB, V7X_LANES), lambda i, c, info: (i, 0)),
        scratch_shapes=[pltpu.VMEM((tm, D), F32)],
    )
    blocks = (_nbytes((tm * IN_SUB, W), F32) + 3 * _nbytes((D, ck), BF16) + _nbytes((tm, D), F32))
    temps = 3 * _nbytes((tm, ck), F32) + 2 * _nbytes((tm, D), F32)
    return pl.pallas_call(
        functools.partial(_expert_kernel, n_chunks=n_chunks),
        grid_spec=grid_spec,
        out_shape=jax.ShapeDtypeStruct((R * OUT_SUB, V7X_LANES), F32),
        compiler_params=_params(_vmem_limit(blocks, _nbytes((tm, D), F32), temps), 2),
        name="expert_ffn",
    )(tile_info, xs, w_gate, w_up, w_down)


def _combine_kernel(pos0_ref, pos1_ref, npos0_ref, npos1_ref, gate_ref, h_ref, g_ref, os_hbm, o_ref,
                    a0_ref, b0_ref, a1_ref, b1_ref, sem):
    tt = COMBINE_TILE
    step = pl.program_id(0)
    bufs = ((a0_ref, b0_ref), (a1_ref, b1_ref))

    def issue(pos_refs, t0, slot):
        def body(t, carry):
            for k in range(TOP_K):
                _slab_copy(os_hbm, pos_refs[k][t0 + t], bufs[slot][k], t, OUT_SUB,
                           sem.at[slot]).start(priority=k)
            return carry
        lax.fori_loop(0, tt, body, 0, unroll=8)

    def drain(slot):
        def body(t, carry):
            for k in range(TOP_K):
                _slab_copy(os_hbm, 0, bufs[slot][k], 0, OUT_SUB, sem.at[slot]).wait()
            return carry
        lax.fori_loop(0, tt, body, 0, unroll=8)

    def finish(slot):
        rs = slice(slot * tt, (slot + 1) * tt)
        rows = lambda ref: jnp.concatenate(
            [ref[pl.ds(j, tt, stride=OUT_SUB), :] for j in range(OUT_SUB)], axis=1)
        a_ref, b_ref = bufs[slot]
        f = gate_ref[rs, 0:1] * rows(a_ref) + gate_ref[rs, 1:2] * rows(b_ref)
        o_ref[rs, :] = h_ref[rs, :] + _rmsnorm(f, g_ref[...])

    @pl.when(step == 0)
    def _():
        issue((pos0_ref, pos1_ref), 0, 0)
    issue((pos0_ref, pos1_ref), tt, 1)
    drain(0)
    finish(0)

    @pl.when(step + 1 < pl.num_programs(0))
    def _():
        issue((npos0_ref, npos1_ref), 0, 0)
    drain(1)
    finish(1)


def _combine(pos, gate_t, h1, g_post, o_sorted):
    T, D = h1.shape
    tt = COMBINE_TILE
    n_steps = T // (2 * tt)
    row = lambda i: (i, 0)
    nxt = lambda i: (jnp.minimum(2 * i + 2, 2 * n_steps - 1),)
    blocks = (_nbytes((2 * tt, V7X_LANES), F32) + 2 * _nbytes((2 * tt, D), F32))
    scratch = 4 * _nbytes((tt, D), F32)
    temps = 3 * _nbytes((tt, D), F32)
    slab_buf = pltpu.VMEM((tt * OUT_SUB, V7X_LANES), F32)
    return pl.pallas_call(
        _combine_kernel,
        grid=(n_steps,),
        in_specs=[
            pl.BlockSpec((2 * tt,), lambda i: (i,), memory_space=pltpu.SMEM),
            pl.BlockSpec((2 * tt,), lambda i: (i,), memory_space=pltpu.SMEM),
            pl.BlockSpec((tt,), nxt, memory_space=pltpu.SMEM),
            pl.BlockSpec((tt,), nxt, memory_space=pltpu.SMEM),
            pl.BlockSpec((2 * tt, TOP_K), row),
            pl.BlockSpec((2 * tt, D), row),
            pl.BlockSpec((1, D), lambda i: (0, 0)),
            pl.BlockSpec(memory_space=pl.ANY),
        ],
        out_specs=pl.BlockSpec((2 * tt, D), row),
        out_shape=jax.ShapeDtypeStruct((T, D), F32),
        scratch_shapes=[slab_buf, slab_buf, slab_buf, slab_buf, pltpu.SemaphoreType.DMA((2,))],
        compiler_params=_params(_vmem_limit(blocks, scratch, temps), 1),
        name="combine",
    )(pos[0], pos[1], pos[0], pos[1], gate_t, h1, g_post, o_sorted)


def _moe_ffn(yrow, logits_t, h1, g_post, w_gate, w_up, w_down, tri):
    T = h1.shape[0]
    n_tiles_max = (TOP_K * T) // EXPERT_ROW_TILE + N_EXPERTS
    assert n_tiles_max < TILE_INFO_LEN
    pos, gate, info = _route(logits_t, tri)
    xs = _dispatch(pos, yrow, n_tiles_max * EXPERT_ROW_TILE)
    o_sorted = _expert_ffn(info.reshape(TILE_INFO_LEN), xs, w_gate, w_up, w_down, n_chunks=2)
    return _combine(pos, gate.T, h1, g_post, o_sorted)


def _rope_inv_freq_lanes():
    half = ROPE_DIM // 2
    inv_freq = jnp.power(jnp.float32(ROPE_THETA),
                         -jnp.arange(0, ROPE_DIM, 2, dtype=F32) / ROPE_DIM)
    lane = np.arange(V7X_LANES) % HEAD_DIM
    idx = np.where(lane < half, lane, lane - half) % half
    return jnp.where(jnp.asarray(lane < ROPE_DIM), inv_freq[idx], 0.0).reshape(1, V7X_LANES)


def kernel(x, positions, pre_mix_norm, w_in, attn_sinks, sgu_ln_g, sgu_ln_b, sgu_w, sgu_b, attn_out_norm, sgu_out_norm, w_out, post_mix_norm, pre_ffn_norm, post_ffn_norm, ffn_w_gate, ffn_w_up, ffn_w_down, moe_router, moe_w_gate, moe_w_up, moe_w_down):
    B, S, D = x.shape
    T = B * S
    depth = pre_mix_norm.shape[0]
    n_q_heads = attn_sinks.shape[1]
    attn_w = n_q_heads * HEAD_DIM
    kv_w = (n_q_heads // Q_PER_KV) * HEAD_DIM
    sgu_width = sgu_ln_g.shape[1]
    assert kv_w == V7X_LANES and attn_w % V7X_LANES == 0 and sgu_width % V7X_LANES == 0
    assert S % TOKEN_TILE == 0 and T % DISPATCH_TILE == 0 and T % (2 * COMBINE_TILE) == 0
    assert sgu_w.shape[2] == BLOCK
    assert moe_router.shape[-1] == N_EXPERTS

    invf = _rope_inv_freq_lanes()
    main_w = attn_w + 2 * kv_w + sgu_width
    lane_rep = lambda a: jnp.broadcast_to(a[:, None], (a.shape[0], V7X_LANES))
    tri = jnp.asarray(np.triu(np.ones((ROUTE_CHUNK, ROUTE_CHUNK), np.float32)), BF16)
    row2 = lambda a: a.reshape(1, -1)

    h = x.reshape(T, D)
    cos, sin = _rope_tables(positions.reshape(T, 1), invf)
    for layer in range(depth):
        w_l = w_in[layer].astype(BF16)
        q, kv, u, vn_t = _in_proj(
            h, cos, sin, row2(pre_mix_norm[layer]), w_l[:, :main_w], w_l[:, main_w:].T,
            lane_rep(sgu_ln_g[layer]), lane_rep(sgu_ln_b[layer]),
            attn_w=attn_w, kv_w=kv_w, sgu_w=sgu_width)
        is_moe = layer % 2 == 1
        i = layer // 2
        router = None
        if is_moe:
            wr = moe_router[i].T
            wr_hi = wr.astype(BF16)
            router = (wr_hi, (wr - wr_hi.astype(F32)).astype(BF16))
        outs = _mix(
            attn_sinks[layer], q, kv, u, vn_t, h,
            jnp.swapaxes(sgu_w[layer], 1, 2).astype(BF16), sgu_b[layer],
            row2(attn_out_norm[layer]), row2(sgu_out_norm[layer]), w_out[layer].astype(BF16),
            row2(post_mix_norm[layer]), row2(pre_ffn_norm[layer]), router, batch=B, seq=S)
        g_post = row2(post_ffn_norm[layer])
        if is_moe:
            h1, yrow, logits_t = outs
            h = _moe_ffn(yrow, logits_t, h1, g_post, moe_w_gate[i].astype(BF16),
                         moe_w_up[i].astype(BF16), moe_w_down[i].astype(BF16), tri)
        else:
            h1, y = outs
            h = _dense_ffn(y, h1, ffn_w_gate[i].astype(BF16), ffn_w_up[i].astype(BF16),
                           ffn_w_down[i].astype(BF16), g_post)
    return h.reshape(B, S, D)
```

```python
import functools

import numpy as np
import jax
import jax.numpy as jnp
from jax import lax
from jax.experimental import pallas as pl
from jax.experimental.pallas import tpu as pltpu

F32 = jnp.float32
BF16 = jnp.bfloat16
I32 = jnp.int32

V7X_LANES = 128
V7X_SUBLANES = 8
V7X_VMEM_BUDGET_BYTES = 56 * 1024 * 1024

HEAD_DIM = 64
Q_PER_KV = 4
WINDOW = 128
BLOCK = 128
ROPE_DIM = HEAD_DIM // 4
ROPE_THETA = 500000.0
N_EXPERTS = 8
TOP_K = 2
EPS = 1e-6
NEG_BIG = -1e30

TOKEN_TILE = 512
BLOCKS_PER_TILE = TOKEN_TILE // BLOCK
IN_PROJ_SUBTILE = 256
MIX_OUT_SUBTILE = 256
EXPERT_ROW_TILE = 512
ROUTE_CHUNK = 512
DISPATCH_TILE = 1024
COMBINE_TILE = 512
TILE_INFO_LEN = 256
IN_SUB = 8
OUT_SUB = 8


def _vmem_limit(block_bytes, scratch_bytes=0, temp_bytes=0):
    need = 2 * int(block_bytes) + int(scratch_bytes) + int(temp_bytes)
    return int(min(max(need, 16 * 1024 * 1024), V7X_VMEM_BUDGET_BYTES))


def _nbytes(shape, dtype):
    return int(np.prod(shape)) * jnp.dtype(dtype).itemsize


def _params(vmem_bytes, n_grid_dims):
    return pltpu.CompilerParams(
        dimension_semantics=("arbitrary",) * n_grid_dims,
        vmem_limit_bytes=vmem_bytes,
    )


def _rmsnorm(x, g):
    return x * lax.rsqrt(jnp.mean(x * x, axis=-1, keepdims=True) + EPS) * g


def _gelu_tanh(x):
    c = np.float32(np.sqrt(2.0 / np.pi))
    return x * (0.5 * (1.0 + jnp.tanh(c * (x + 0.044715 * (x * x * x)))))


def _silu(x):
    return x * (1.0 / (1.0 + jnp.exp(-x)))


def _split_bf16(x):
    hi = x.astype(BF16)
    lo = (x - hi.astype(F32)).astype(BF16)
    return hi, lo


def _dot(a, b):
    return jnp.dot(a, b, preferred_element_type=F32)


def _dot_nt(a, b):
    return lax.dot_general(a, b, (((1,), (1,)), ((), ())), preferred_element_type=F32)


def _rope_table_kernel(pos_ref, invf_ref, cos_ref, sin_ref):
    ang = pos_ref[...].astype(F32) * invf_ref[...]
    cos_ref[...] = jnp.cos(ang)
    sin_ref[...] = jnp.sin(ang)


def _rope_tables(pos, invf):
    T = pos.shape[0]
    tm = TOKEN_TILE
    row = lambda i: (i, 0)
    blocks = 3 * _nbytes((tm, V7X_LANES), F32)
    return pl.pallas_call(
        _rope_table_kernel,
        grid=(T // tm,),
        in_specs=[pl.BlockSpec((tm, 1), row), pl.BlockSpec((1, V7X_LANES), lambda i: (0, 0))],
        out_specs=[pl.BlockSpec((tm, V7X_LANES), row), pl.BlockSpec((tm, V7X_LANES), row)],
        out_shape=[jax.ShapeDtypeStruct((T, V7X_LANES), F32), jax.ShapeDtypeStruct((T, V7X_LANES), F32)],
        compiler_params=_params(_vmem_limit(blocks, 0, 8 * blocks), 1),
        name="rope_tables",
    )(pos, invf)


def _in_proj_kernel(h_ref, cos_ref, sin_ref, g_ref, w_ref, wvt_ref, lng_ref, lnb_ref,
                    q_ref, kv_ref, u_ref, vnt_ref, *, attn_w, kv_w, sgu_w):
    lane = lax.broadcasted_iota(I32, (1, V7X_LANES), 1)
    hl = lane % HEAD_DIM
    low = lane < HEAD_DIM
    half = ROPE_DIM // 2
    scale = np.float32(HEAD_DIM ** -0.5)
    tm = h_ref.shape[0]

    for r0 in range(0, tm, IN_PROJ_SUBTILE):
        rs = slice(r0, r0 + IN_PROJ_SUBTILE)
        hn = _rmsnorm(h_ref[rs, :], g_ref[...]).astype(BF16)
        z = _dot(hn, w_ref[...])

        cos = cos_ref[rs, :]
        sin = sin_ref[rs, :]
        sin_a = jnp.where(hl < half, -sin, 0.0)
        sin_b = jnp.where((hl >= half) & (hl < ROPE_DIM), sin, 0.0)

        def rope(t):
            return (t * cos + pltpu.roll(t, V7X_LANES - half, 1) * sin_a
                    + pltpu.roll(t, half, 1) * sin_b)

        for c in range(attn_w // V7X_LANES):
            sl = slice(c * V7X_LANES, (c + 1) * V7X_LANES)
            q_ref[rs, sl] = (rope(z[:, sl]) * scale).astype(BF16)

        k = rope(z[:, attn_w:attn_w + kv_w])
        v = z[:, attn_w + kv_w:attn_w + 2 * kv_w]
        kr = pltpu.roll(k, HEAD_DIM, 1)
        vr = pltpu.roll(v, HEAD_DIM, 1)
        kv_ref[rs, 0 * V7X_LANES:1 * V7X_LANES] = jnp.where(low, k, kr).astype(BF16)
        kv_ref[rs, 1 * V7X_LANES:2 * V7X_LANES] = jnp.where(low, kr, k).astype(BF16)
        kv_ref[rs, 2 * V7X_LANES:3 * V7X_LANES] = jnp.where(low, v, vr).astype(BF16)
        kv_ref[rs, 3 * V7X_LANES:4 * V7X_LANES] = jnp.where(low, vr, v).astype(BF16)

        o = attn_w + 2 * kv_w
        u_ref[rs, :] = _gelu_tanh(z[:, o:o + sgu_w])

        zvt = _gelu_tanh(_dot_nt(wvt_ref[...], hn))
        g3 = zvt.reshape(sgu_w // HEAD_DIM, HEAD_DIM, IN_PROJ_SUBTILE)
        d = g3 - jnp.mean(g3, axis=1, keepdims=True)
        var = jnp.mean(d * d, axis=1, keepdims=True)
        vn = (d * lax.rsqrt(var + EPS)).reshape(sgu_w, IN_PROJ_SUBTILE)
        for c in range(IN_PROJ_SUBTILE // V7X_LANES):
            sl = slice(c * V7X_LANES, (c + 1) * V7X_LANES)
            vnt_ref[:, r0 + c * V7X_LANES:r0 + (c + 1) * V7X_LANES] = (
                vn[:, sl] * lng_ref[...] + lnb_ref[...]).astype(BF16)


def _in_proj(h, cos, sin, g_pre, w_main, w_zv_t, ln_g, ln_b, *, attn_w, kv_w, sgu_w):
    T, D = h.shape
    tm = TOKEN_TILE
    in_w = w_main.shape[1] + sgu_w
    row = lambda i: (i, 0)
    const = lambda i: (0, 0)
    blocks = (_nbytes((tm, D), F32) + 2 * _nbytes((tm, V7X_LANES), F32) + _nbytes((D, in_w), BF16)
              + _nbytes((tm, attn_w), BF16) + _nbytes((tm, 4 * V7X_LANES), BF16)
              + _nbytes((tm, sgu_w), F32) + _nbytes((tm, sgu_w), BF16))
    temps = 6 * _nbytes((tm, in_w), F32)
    kern = functools.partial(_in_proj_kernel, attn_w=attn_w, kv_w=kv_w, sgu_w=sgu_w)
    return pl.pallas_call(
        kern,
        grid=(T // tm,),
        in_specs=[
            pl.BlockSpec((tm, D), row),
            pl.BlockSpec((tm, V7X_LANES), row),
            pl.BlockSpec((tm, V7X_LANES), row),
            pl.BlockSpec((1, D), const),
            pl.BlockSpec((D, in_w - sgu_w), const),
            pl.BlockSpec((sgu_w, D), const),
            pl.BlockSpec((sgu_w, V7X_LANES), const),
            pl.BlockSpec((sgu_w, V7X_LANES), const),
        ],
        out_specs=[
            pl.BlockSpec((tm, attn_w), row),
            pl.BlockSpec((tm, 4 * V7X_LANES), row),
            pl.BlockSpec((tm, sgu_w), row),
            pl.BlockSpec((sgu_w, tm), lambda i: (0, i)),
        ],
        out_shape=[
            jax.ShapeDtypeStruct((T, attn_w), BF16),
            jax.ShapeDtypeStruct((T, 4 * V7X_LANES), BF16),
            jax.ShapeDtypeStruct((T, sgu_w), F32),
            jax.ShapeDtypeStruct((sgu_w, T), BF16),
        ],
        compiler_params=_params(_vmem_limit(blocks, 0, temps), 1),
        name="in_proj",
    )(h, cos, sin, g_pre, w_main, w_zv_t, ln_g, ln_b)


def _mix_kernel(sink_ref, q_ref, kv_ref, kvp_ref, u_ref, vnt_ref, h_ref, wst_ref, bs_ref,
                ga_ref, gs_ref, wo_ref, gpost_ref, gffn_ref, *rest, moe, attn_w, sgu_w):
    if moe:
        wr_hi_ref, wr_lo_ref, h1_ref, yrow_ref, lg_ref, merged_ref = rest
    else:
        h1_ref, y_ref, merged_ref = rest

    first_tile = pl.program_id(1) == 0
    n_pairs = attn_w // V7X_LANES
    band = 2 * BLOCK

    n_kvh = n_pairs // 2
    row2 = lax.broadcasted_iota(I32, (2 * BLOCK, 2 * band), 0)
    row = row2 % BLOCK
    col = lax.broadcasted_iota(I32, (2 * BLOCK, 2 * band), 1) % band
    in_window = (col > row) & (col <= row + WINDOW)
    upper = lax.broadcasted_iota(I32, (2 * BLOCK, 1), 0) < BLOCK
    lane = lax.broadcasted_iota(I32, (1, V7X_LANES), 1)
    low = lane < HEAD_DIM
    zero_bf = jnp.zeros((), BF16)
    n_chunks = sgu_w // V7X_LANES

    causal_t = (lax.broadcasted_iota(I32, (BLOCK, BLOCK), 0)
                <= lax.broadcasted_iota(I32, (BLOCK, BLOCK), 1))
    top = lax.broadcasted_iota(I32, (2 * HEAD_DIM, 1), 0) < HEAD_DIM
    mixed_t = []
    for g in range(2 * n_chunks):
        lhs = jnp.concatenate(
            [vnt_ref[g * HEAD_DIM:(g + 1) * HEAD_DIM, b * BLOCK:(b + 1) * BLOCK]
             for b in range(BLOCKS_PER_TILE)], axis=0)
        mixed_t.append(_dot(lhs, jnp.where(causal_t, wst_ref[g], zero_bf)))
    bias_t = [jnp.where(top, bs_ref[2 * c:2 * c + 1, :], bs_ref[2 * c + 1:2 * c + 2, :])
              for c in range(n_chunks)]

    for blk in range(BLOCKS_PER_TILE):
        rows = slice(blk * BLOCK, (blk + 1) * BLOCK)
        if blk == 0:
            kv_prev = kvp_ref[...]
            mask = in_window & ((col >= BLOCK) | jnp.logical_not(first_tile))
        else:
            kv_prev = kv_ref[(blk - 1) * BLOCK:blk * BLOCK, :]
            mask = in_window
        kv_band = jnp.concatenate([kv_prev, kv_ref[rows, :]], axis=0)

        attn_parts = [None] * n_pairs
        for kvh in range(n_kvh):
            kk = kv_band[:, kvh * V7X_LANES:(kvh + 1) * V7X_LANES]
            vv = kv_band[:, (n_kvh + kvh) * V7X_LANES:(n_kvh + 1 + kvh) * V7X_LANES]
            k2 = jnp.concatenate([jnp.where(low, kk, zero_bf), jnp.where(low, zero_bf, kk)], axis=0)
            v2 = jnp.concatenate([jnp.where(low, vv, zero_bf), jnp.where(low, zero_bf, vv)], axis=0)
            p0, p1 = 2 * kvh, 2 * kvh + 1
            qq = jnp.concatenate([q_ref[rows, p0 * V7X_LANES:(p0 + 1) * V7X_LANES],
                                  q_ref[rows, p1 * V7X_LANES:(p1 + 1) * V7X_LANES]], axis=0)
            s = jnp.where(mask, _dot_nt(qq, k2), NEG_BIG)
            probs, inv = [], []
            for hh in range(2):
                sink = jnp.where(upper, sink_ref[2 * p0 + hh], sink_ref[2 * p1 + hh])
                sh = s[:, hh * band:(hh + 1) * band]
                m = jnp.maximum(jnp.max(sh, axis=-1, keepdims=True), sink)
                p = jnp.exp(sh - m)
                denom = jnp.sum(p, axis=-1, keepdims=True) + jnp.exp(sink - m)
                probs.append(p.astype(BF16))
                inv.append(1.0 / denom)
            o = _dot(jnp.concatenate(probs, axis=1), v2) * jnp.where(low, inv[0], inv[1])
            attn_parts[p0] = o[:BLOCK]
            attn_parts[p1] = o[BLOCK:]
        attn = jnp.concatenate(attn_parts, axis=1)

        sgu_parts = []
        for c in range(n_chunks):
            drows = slice(blk * HEAD_DIM, (blk + 1) * HEAD_DIM)
            pair_t = jnp.concatenate([mixed_t[2 * c][drows], mixed_t[2 * c + 1][drows]], axis=0)
            sgu_parts.append((pair_t + bias_t[c]).T)
        sgu = u_ref[rows, :] * jnp.concatenate(sgu_parts, axis=1)

        merged_ref[rows, 0:attn_w] = _rmsnorm(attn, ga_ref[...]).astype(BF16)
        merged_ref[rows, attn_w:attn_w + sgu_w] = _rmsnorm(sgu, gs_ref[...]).astype(BF16)

    ys = []
    for r0 in range(0, h_ref.shape[0], MIX_OUT_SUBTILE):
        rs = slice(r0, r0 + MIX_OUT_SUBTILE)
        proj = _dot(merged_ref[rs, :], wo_ref[...])
        h1 = h_ref[rs, :] + _rmsnorm(proj, gpost_ref[...])
        h1_ref[rs, :] = h1
        y = _rmsnorm(h1, gffn_ref[...])
        if not moe:
            y_ref[rs, :] = y.astype(BF16)
        else:
            ys.append(y)
    if moe:
        y = jnp.concatenate(ys, axis=0)
        for j in range(IN_SUB):
            yrow_ref[pl.ds(j, y.shape[0], stride=IN_SUB), :] = y[:, j * V7X_LANES:(j + 1) * V7X_LANES]
        y_hi, y_lo = _split_bf16(y)
        wr_hi = wr_hi_ref[...]
        n_e = wr_hi.shape[0]
        both = _dot_nt(jnp.concatenate([wr_hi, wr_lo_ref[...]], axis=0), y_hi)
        lg_ref[...] = both[:n_e] + _dot_nt(wr_hi, y_lo) + both[n_e:]


def _mix(sinks, q, kv, u, vn_t, h, w_s_t, b_s, g_attn, g_sgu, w_out, g_post, g_ffn, router, *,
         batch, seq):
    T, D = h.shape
    tq = TOKEN_TILE
    attn_w = q.shape[1]
    sgu_w = u.shape[1]
    tiles_per_seq = seq // tq
    blocks_per_seq = seq // BLOCK
    moe = router is not None
    n_groups = w_s_t.shape[0]

    row = lambda b, j: (b * tiles_per_seq + j, 0)
    prev = lambda b, j: (b * blocks_per_seq + jnp.maximum(j * BLOCKS_PER_TILE - 1, 0), 0)
    const2 = lambda b, j: (0, 0)
    const3 = lambda b, j: (0, 0, 0)

    in_specs = [
        pl.BlockSpec(memory_space=pltpu.SMEM),
        pl.BlockSpec((tq, attn_w), row),
        pl.BlockSpec((tq, 4 * V7X_LANES), row),
        pl.BlockSpec((BLOCK, 4 * V7X_LANES), prev),
        pl.BlockSpec((tq, sgu_w), row),
        pl.BlockSpec((sgu_w, tq), lambda b, j: (0, b * tiles_per_seq + j)),
        pl.BlockSpec((tq, D), row),
        pl.BlockSpec((n_groups, BLOCK, BLOCK), const3),
        pl.BlockSpec((n_groups, BLOCK), const2),
        pl.BlockSpec((1, attn_w), const2),
        pl.BlockSpec((1, sgu_w), const2),
        pl.BlockSpec((D, D), const2),
        pl.BlockSpec((1, D), const2),
        pl.BlockSpec((1, D), const2),
    ]
    args = [sinks, q, kv, kv, u, vn_t, h, w_s_t, b_s, g_attn, g_sgu, w_out, g_post, g_ffn]
    out_specs = [pl.BlockSpec((tq, D), row)]
    out_shape = [jax.ShapeDtypeStruct((T, D), F32)]
    blocks = (_nbytes((tq, attn_w), BF16) + 2 * _nbytes((tq, 4 * V7X_LANES), BF16)
              + _nbytes((tq, sgu_w), F32) + _nbytes((tq, sgu_w), BF16) + 2 * _nbytes((tq, D), F32)
              + _nbytes((n_groups, BLOCK, BLOCK), BF16) + _nbytes((D, D), BF16)
              + _nbytes((tq, D), F32))
    if moe:
        n_e = router[0].shape[0]
        in_specs += [pl.BlockSpec((n_e, D), const2), pl.BlockSpec((n_e, D), const2)]
        args += list(router)
        out_specs += [pl.BlockSpec((tq * IN_SUB, V7X_LANES), row),
                      pl.BlockSpec((n_e, tq), lambda b, j: (0, b * tiles_per_seq + j))]
        out_shape += [jax.ShapeDtypeStruct((T * IN_SUB, V7X_LANES), F32),
                      jax.ShapeDtypeStruct((n_e, T), F32)]
    else:
        out_specs += [pl.BlockSpec((tq, D), row)]
        out_shape += [jax.ShapeDtypeStruct((T, D), BF16)]
    scratch = _nbytes((tq, D), BF16)
    temps = 8 * _nbytes((tq, D), F32)
    kern = functools.partial(_mix_kernel, moe=moe, attn_w=attn_w, sgu_w=sgu_w)
    return pl.pallas_call(
        kern,
        grid=(batch, tiles_per_seq),
        in_specs=in_specs,
        out_specs=out_specs,
        out_shape=out_shape,
        scratch_shapes=[pltpu.VMEM((tq, D), BF16)],
        compiler_params=_params(_vmem_limit(blocks, scratch, temps), 2),
        name="mix_moe" if moe else "mix_dense",
    )(*args)


def _dense_ffn_kernel(y_ref, h_ref, wg_ref, wu_ref, wd_ref, g_ref, o_ref, *, chunk):
    y = y_ref[...]
    d_ff = wg_ref.shape[1]
    acc = jnp.zeros(o_ref.shape, F32)
    for c in range(d_ff // chunk):
        sl = slice(c * chunk, (c + 1) * chunk)
        a = _silu(_dot(y, wg_ref[:, sl])) * _dot(y, wu_ref[:, sl])
        acc = acc + _dot(a.astype(BF16), wd_ref[sl, :])
    o_ref[...] = h_ref[...] + _rmsnorm(acc, g_ref[...])


def _dense_ffn(y, h1, w_gate, w_up, w_down, g_post):
    T, D = h1.shape
    d_ff = w_gate.shape[1]
    tm = TOKEN_TILE
    chunk = 2 * V7X_LANES
    assert d_ff % chunk == 0
    row = lambda i: (i, 0)
    const = lambda i: (0, 0)
    blocks = (_nbytes((tm, D), BF16) + 2 * _nbytes((tm, D), F32) + 3 * _nbytes((D, d_ff), BF16))
    temps = 4 * _nbytes((tm, D), F32)
    return pl.pallas_call(
        functools.partial(_dense_ffn_kernel, chunk=chunk),
        grid=(T // tm,),
        in_specs=[
            pl.BlockSpec((tm, D), row),
            pl.BlockSpec((tm, D), row),
            pl.BlockSpec((D, d_ff), const),
            pl.BlockSpec((D, d_ff), const),
            pl.BlockSpec((d_ff, D), const),
            pl.BlockSpec((1, D), const),
        ],
        out_specs=pl.BlockSpec((tm, D), row),
        out_shape=jax.ShapeDtypeStruct((T, D), F32),
        compiler_params=_params(_vmem_limit(blocks, 0, temps), 1),
        name="dense_ffn",
    )(y, h1, w_gate, w_up, w_down, g_post)


def _route_kernel(lg_ref, tri_ref, pos_ref, gate_ref, info_ref, oh_ref, rank_ref):
    n_e, T = lg_ref.shape
    lg = lg_ref[...]
    eidx = lax.broadcasted_iota(I32, (n_e, T), 0)
    m1 = jnp.max(lg, axis=0, keepdims=True)
    i1 = jnp.min(jnp.where(lg == m1, eidx, n_e), axis=0, keepdims=True)
    oh1 = eidx == i1
    lg2 = jnp.where(oh1, -jnp.inf, lg)
    m2 = jnp.max(lg2, axis=0, keepdims=True)
    i2 = jnp.min(jnp.where(lg2 == m2, eidx, n_e), axis=0, keepdims=True)
    oh2 = eidx == i2
    e2 = jnp.exp(m2 - m1)
    gate_ref[0:1, :] = 1.0 / (1.0 + e2)
    gate_ref[1:2, :] = e2 / (1.0 + e2)

    oh_ref[...] = jnp.where(oh1 | oh2, 1.0, 0.0)

    tri = tri_ref[...]

    def body(i, carry):
        start = pl.multiple_of(i * ROUTE_CHUNK, ROUTE_CHUNK)
        ohc = oh_ref[:, pl.ds(start, ROUTE_CHUNK)]
        inc = _dot(ohc.astype(BF16), tri)
        rank_ref[:, pl.ds(start, ROUTE_CHUNK)] = carry + inc - ohc
        return carry + inc[:, ROUTE_CHUNK - 1:ROUTE_CHUNK]

    count = lax.fori_loop(0, T // ROUTE_CHUNK, body, jnp.zeros((n_e, 1), F32))

    tm = np.float32(EXPERT_ROW_TILE)
    n_tiles = jnp.floor((count + (tm - 1.0)) * (1.0 / tm))
    e_col = lax.broadcasted_iota(I32, (n_e, 1), 0)
    first_tile = jnp.zeros((n_e, 1), F32)
    for e in range(n_e - 1):
        nt_e = jnp.sum(jnp.where(e_col == e, n_tiles, 0.0), axis=0, keepdims=True)
        first_tile = first_tile + jnp.where(e_col > e, nt_e, 0.0)
    row0 = first_tile * tm + rank_ref[...]
    pos_ref[0:1, :] = jnp.sum(jnp.where(oh1, row0, 0.0), axis=0, keepdims=True).astype(I32)
    pos_ref[1:2, :] = jnp.sum(jnp.where(oh2, row0, 0.0), axis=0, keepdims=True).astype(I32)

    end_tile = (first_tile + n_tiles).astype(I32)
    tile = lax.broadcasted_iota(I32, (n_e, TILE_INFO_LEN), 1)
    owner = jnp.sum(jnp.where(end_tile <= tile, 1, 0), axis=0, keepdims=True)
    total = jnp.sum(n_tiles, axis=0, keepdims=True).astype(I32)
    lane = lax.broadcasted_iota(I32, (1, TILE_INFO_LEN), 1)
    info_ref[...] = jnp.where(lane == TILE_INFO_LEN - 1, total, jnp.minimum(owner, n_e - 1))


def _route(logits_t, tri):
    n_e, T = logits_t.shape
    blocks = (_nbytes((n_e, T), F32) + _nbytes((ROUTE_CHUNK, ROUTE_CHUNK), BF16)
              + _nbytes((8, T), I32) + _nbytes((8, T), F32))
    scratch = 2 * _nbytes((n_e, T), F32)
    temps = 12 * _nbytes((n_e, T), F32)
    return pl.pallas_call(
        _route_kernel,
        out_shape=[
            jax.ShapeDtypeStruct((TOP_K, T), I32),
            jax.ShapeDtypeStruct((TOP_K, T), F32),
            jax.ShapeDtypeStruct((1, TILE_INFO_LEN), I32),
        ],
        scratch_shapes=[pltpu.VMEM((n_e, T), F32), pltpu.VMEM((n_e, T), F32)],
        compiler_params=pltpu.CompilerParams(vmem_limit_bytes=_vmem_limit(blocks, scratch, temps)),
        name="route",
    )(logits_t, tri)


def _slab_copy(src, src_row, dst, dst_row, sub, sem):
    s0 = pl.multiple_of(src_row * sub, sub)
    d0 = pl.multiple_of(dst_row * sub, sub)
    return pltpu.make_async_copy(src.at[pl.ds(s0, sub)], dst.at[pl.ds(d0, sub)], sem)


def _dispatch_kernel(pos0_ref, pos1_ref, yrow_ref, xs_in_hbm, xs_hbm, sem):
    del xs_in_hbm
    tt = pos0_ref.shape[0]
    pos_refs = (pos0_ref, pos1_ref)

    def issue(t, carry):
        for k in range(TOP_K):
            _slab_copy(yrow_ref, t, xs_hbm, pos_refs[k][t], IN_SUB, sem).start(priority=k)
        return carry

    def drain(t, carry):
        for k in range(TOP_K):
            _slab_copy(yrow_ref, 0, xs_hbm, 0, IN_SUB, sem).wait()
        return carry

    lax.fori_loop(0, tt, issue, 0, unroll=8)
    lax.fori_loop(0, tt, drain, 0, unroll=8)


def _dispatch(pos, yrow, n_rows):
    T = yrow.shape[0] // IN_SUB
    W = V7X_LANES
    tt = DISPATCH_TILE
    xs0 = jnp.zeros((n_rows * IN_SUB, W), F32)
    return pl.pallas_call(
        _dispatch_kernel,
        grid=(T // tt,),
        in_specs=[
            pl.BlockSpec((tt,), lambda i: (i,), memory_space=pltpu.SMEM),
            pl.BlockSpec((tt,), lambda i: (i,), memory_space=pltpu.SMEM),
            pl.BlockSpec((tt * IN_SUB, W), lambda i: (i, 0)),
            pl.BlockSpec(memory_space=pl.ANY),
        ],
        out_specs=pl.BlockSpec(memory_space=pl.ANY),
        out_shape=jax.ShapeDtypeStruct((n_rows * IN_SUB, W), F32),
        scratch_shapes=[pltpu.SemaphoreType.DMA(())],
        input_output_aliases={3: 0},
        compiler_params=pltpu.CompilerParams(
            dimension_semantics=("arbitrary",), has_side_effects=True,
            vmem_limit_bytes=_vmem_limit(_nbytes((tt * IN_SUB, W), F32))),
        name="dispatch",
    )(pos[0], pos[1], yrow, xs0)


def _expert_kernel(info_ref, xs_ref, wg_ref, wu_ref, wd_ref, o_ref, acc_ref, *, n_chunks):
    i = pl.program_id(0)
    c = pl.program_id(1)
    valid = i < info_ref[TILE_INFO_LEN - 1]
    last = c == pl.num_programs(1) - 1
    tm = acc_ref.shape[0]

    def chunk_out():
        x = jnp.concatenate([xs_ref[pl.ds(j, tm, stride=IN_SUB), :] for j in range(IN_SUB)],
                            axis=1).astype(BF16)
        a = _silu(_dot(x, wg_ref[0])) * _dot(x, wu_ref[0])
        return _dot(a.astype(BF16), wd_ref[0])

    def write_rows(res):
        for j in range(OUT_SUB):
            o_ref[pl.ds(j, tm, stride=OUT_SUB), :] = res[:, j * V7X_LANES:(j + 1) * V7X_LANES]

    @pl.when(valid & (c == 0))
    def _():
        acc_ref[...] = chunk_out()

    if n_chunks > 2:
        @pl.when(valid & (c > 0) & jnp.logical_not(last))
        def _():
            acc_ref[...] += chunk_out()

    @pl.when(valid & last)
    def _():
        write_rows(acc_ref[...] + chunk_out())

    @pl.when(jnp.logical_not(valid) & last)
    def _():
        write_rows(jnp.zeros(acc_ref.shape, F32))


def _expert_ffn(tile_info, xs, w_gate, w_up, w_down, n_chunks):
    R = xs.shape[0] // IN_SUB
    W = V7X_LANES
    n_e, D, d_ff = w_gate.shape
    assert D == OUT_SUB * V7X_LANES
    tm = EXPERT_ROW_TILE
    ck = d_ff // n_chunks
    last = TILE_INFO_LEN - 1

    def tile_of(i, info):
        return jnp.minimum(i, info[last] - 1)

    def chunk_of(i, c, info):
        return jnp.where(i < info[last], c, n_chunks - 1)

    grid_spec = pltpu.PrefetchScalarGridSpec(
        num_scalar_prefetch=1,
        grid=(R // tm, n_chunks),
        in_specs=[
            pl.BlockSpec((tm * IN_SUB, W), lambda i, c, info: (tile_of(i, info), 0)),
            pl.BlockSpec((1, D, ck), lambda i, c, info: (info[tile_of(i, info)], 0, chunk_of(i, c, info))),
            pl.BlockSpec((1, D, ck), lambda i, c, info: (info[tile_of(i, info)], 0, chunk_of(i, c, info))),
            pl.BlockSpec((1, ck, D), lambda i, c, info: (info[tile_of(i, info)], chunk_of(i, c, info), 0)),
        ],
        out_specs=pl.BlockSpec((tm * OUT_SUB, V7X_LANES), lambda i, c, info: (i, 0)),
        scratch_shapes=[pltpu.VMEM((tm, D), F32)],
    )
    blocks = (_nbytes((tm * IN_SUB, W), F32) + 3 * _nbytes((D, ck), BF16) + _nbytes((tm, D), F32))
    temps = 3 * _nbytes((tm, ck), F32) + 2 * _nbytes((tm, D), F32)
    return pl.pallas_call(
        functools.partial(_expert_kernel, n_chunks=n_chunks),
        grid_spec=grid_spec,
        out_shape=jax.ShapeDtypeStruct((R * OUT_SUB, V7X_LANES), F32),
        compiler_params=_params(_vmem_limit(blocks, _nbytes((tm, D), F32), temps), 2),
        name="expert_ffn",
    )(tile_info, xs, w_gate, w_up, w_down)


def _combine_kernel(pos0_ref, pos1_ref, npos0_ref, npos1_ref, gate_ref, h_ref, g_ref, os_hbm, o_ref,
                    a0_ref, b0_ref, a1_ref, b1_ref, sem):
    tt = COMBINE_TILE
    step = pl.program_id(0)
    bufs = ((a0_ref, b0_ref), (a1_ref, b1_ref))

    def issue(pos_refs, t0, slot):
        def body(t, carry):
            for k in range(TOP_K):
                _slab_copy(os_hbm, pos_refs[k][t0 + t], bufs[slot][k], t, OUT_SUB,
                           sem.at[slot]).start(priority=k)
            return carry
        lax.fori_loop(0, tt, body, 0, unroll=8)

    def drain(slot):
        def body(t, carry):
            for k in range(TOP_K):
                _slab_copy(os_hbm, 0, bufs[slot][k], 0, OUT_SUB, sem.at[slot]).wait()
            return carry
        lax.fori_loop(0, tt, body, 0, unroll=8)

    def finish(slot):
        rs = slice(slot * tt, (slot + 1) * tt)
        rows = lambda ref: jnp.concatenate(
            [ref[pl.ds(j, tt, stride=OUT_SUB), :] for j in range(OUT_SUB)], axis=1)
        a_ref, b_ref = bufs[slot]
        f = gate_ref[rs, 0:1] * rows(a_ref) + gate_ref[rs, 1:2] * rows(b_ref)
        o_ref[rs, :] = h_ref[rs, :] + _rmsnorm(f, g_ref[...])

    @pl.when(step == 0)
    def _():
        issue((pos0_ref, pos1_ref), 0, 0)
    issue((pos0_ref, pos1_ref), tt, 1)
    drain(0)
    finish(0)

    @pl.when(step + 1 < pl.num_programs(0))
    def _():
        issue((npos0_ref, npos1_ref), 0, 0)
    drain(1)
    finish(1)


def _combine(pos, gate_t, h1, g_post, o_sorted):
    T, D = h1.shape
    tt = COMBINE_TILE
    n_steps = T // (2 * tt)
    row = lambda i: (i, 0)
    nxt = lambda i: (jnp.minimum(2 * i + 2, 2 * n_steps - 1),)
    blocks = (_nbytes((2 * tt, V7X_LANES), F32) + 2 * _nbytes((2 * tt, D), F32))
    scratch = 4 * _nbytes((tt, D), F32)
    temps = 3 * _nbytes((tt, D), F32)
    slab_buf = pltpu.VMEM((tt * OUT_SUB, V7X_LANES), F32)
    return pl.pallas_call(
        _combine_kernel,
        grid=(n_steps,),
        in_specs=[
            pl.BlockSpec((2 * tt,), lambda i: (i,), memory_space=pltpu.SMEM),
            pl.BlockSpec((2 * tt,), lambda i: (i,), memory_space=pltpu.SMEM),
            pl.BlockSpec((tt,), nxt, memory_space=pltpu.SMEM),
            pl.BlockSpec((tt,), nxt, memory_space=pltpu.SMEM),
            pl.BlockSpec((2 * tt, TOP_K), row),
            pl.BlockSpec((2 * tt, D), row),
            pl.BlockSpec((1, D), lambda i: (0, 0)),
            pl.BlockSpec(memory_space=pl.ANY),
        ],
        out_specs=pl.BlockSpec((2 * tt, D), row),
        out_shape=jax.ShapeDtypeStruct((T, D), F32),
        scratch_shapes=[slab_buf, slab_buf, slab_buf, slab_buf, pltpu.SemaphoreType.DMA((2,))],
        compiler_params=_params(_vmem_limit(blocks, scratch, temps), 1),
        name="combine",
    )(pos[0], pos[1], pos[0], pos[1], gate_t, h1, g_post, o_sorted)


def _moe_ffn(yrow, logits_t, h1, g_post, w_gate, w_up, w_down, tri):
    T = h1.shape[0]
    n_tiles_max = (TOP_K * T) // EXPERT_ROW_TILE + N_EXPERTS
    assert n_tiles_max < TILE_INFO_LEN
    pos, gate, info = _route(logits_t, tri)
    xs = _dispatch(pos, yrow, n_tiles_max * EXPERT_ROW_TILE)
    o_sorted = _expert_ffn(info.reshape(TILE_INFO_LEN), xs, w_gate, w_up, w_down, n_chunks=2)
    return _combine(pos, gate.T, h1, g_post, o_sorted)


def _rope_inv_freq_lanes():
    half = ROPE_DIM // 2
    inv_freq = jnp.power(jnp.float32(ROPE_THETA),
                         -jnp.arange(0, ROPE_DIM, 2, dtype=F32) / ROPE_DIM)
    lane = np.arange(V7X_LANES) % HEAD_DIM
    idx = np.where(lane < half, lane, lane - half) % half
    return jnp.where(jnp.asarray(lane < ROPE_DIM), inv_freq[idx], 0.0).reshape(1, V7X_LANES)


def kernel(x, positions, pre_mix_norm, w_in, attn_sinks, sgu_ln_g, sgu_ln_b, sgu_w, sgu_b, attn_out_norm, sgu_out_norm, w_out, post_mix_norm, pre_ffn_norm, post_ffn_norm, ffn_w_gate, ffn_w_up, ffn_w_down, moe_router, moe_w_gate, moe_w_up, moe_w_down):
    B, S, D = x.shape
    T = B * S
    depth = pre_mix_norm.shape[0]
    n_q_heads = attn_sinks.shape[1]
    attn_w = n_q_heads * HEAD_DIM
    kv_w = (n_q_heads // Q_PER_KV) * HEAD_DIM
    sgu_width = sgu_ln_g.shape[1]
    assert kv_w == V7X_LANES and attn_w % V7X_LANES == 0 and sgu_width % V7X_LANES == 0
    assert S % TOKEN_TILE == 0 and T % DISPATCH_TILE == 0 and T % (2 * COMBINE_TILE) == 0
    assert sgu_w.shape[2] == BLOCK
    assert moe_router.shape[-1] == N_EXPERTS

    invf = _rope_inv_freq_lanes()
    main_w = attn_w + 2 * kv_w + sgu_width
    lane_rep = lambda a: jnp.broadcast_to(a[:, None], (a.shape[0], V7X_LANES))
    tri = jnp.asarray(np.triu(np.ones((ROUTE_CHUNK, ROUTE_CHUNK), np.float32)), BF16)
    row2 = lambda a: a.reshape(1, -1)

    h = x.reshape(T, D)
    cos, sin = _rope_tables(positions.reshape(T, 1), invf)
    for layer in range(depth):
        w_l = w_in[layer].astype(BF16)
        q, kv, u, vn_t = _in_proj(
            h, cos, sin, row2(pre_mix_norm[layer]), w_l[:, :main_w], w_l[:, main_w:].T,
            lane_rep(sgu_ln_g[layer]), lane_rep(sgu_ln_b[layer]),
            attn_w=attn_w, kv_w=kv_w, sgu_w=sgu_width)
        is_moe = layer % 2 == 1
        i = layer // 2
        router = None
        if is_moe:
            wr = moe_router[i].T
            wr_hi = wr.astype(BF16)
            router = (wr_hi, (wr - wr_hi.astype(F32)).astype(BF16))
        outs = _mix(
            attn_sinks[layer], q, kv, u, vn_t, h,
            jnp.swapaxes(sgu_w[layer], 1, 2).astype(BF16), sgu_b[layer],
            row2(attn_out_norm[layer]), row2(sgu_out_norm[layer]), w_out[layer].astype(BF16),
            row2(post_mix_norm[layer]), row2(pre_ffn_norm[layer]), router, batch=B, seq=S)
        g_post = row2(post_ffn_norm[layer])
        if is_moe:
            h1, yrow, logits_t = outs
            h = _moe_ffn(yrow, logits_t, h1, g_post, moe_w_gate[i].astype(BF16),
                         moe_w_up[i].astype(BF16), moe_w_down[i].astype(BF16), tri)
        else:
            h1, y = outs
            h = _dense_ffn(y, h1, ffn_w_gate[i].astype(BF16), ffn_w_up[i].astype(BF16),
                           ffn_w_down[i].astype(BF16), g_post)
    return h.reshape(B, S, D)
```
